```python
import jax, jax.numpy as jnp
from jax import lax
import numpy as np

D_MODEL = 1024
BATCH = 2
SEQ = 16384
DEPTH = 2
DEC_BATCH = 8
DEC_SEQ = 2048
PAST_LEN = 128

D_PLE = 256
CHUNK = 64
N_DIR = 2
EPS = 1e-6
GLA_HEADS = 4
GLA_KEY = D_MODEL // 2
GLA_VAL = D_MODEL
GLA_DK = GLA_KEY // GLA_HEADS
GLA_DV = GLA_VAL // GLA_HEADS
GLA_RANK = 16
GLA_TAU = 16.0
SSD_WIDTH = D_MODEL
SSD_HEADDIM = 64
SSD_HEADS = SSD_WIDTH // SSD_HEADDIM
SSD_GROUPS = 4
SSD_HPG = SSD_HEADS // SSD_GROUPS
SSD_STATE = 128
SSD_CONV = 5
SSD_XBC = SSD_WIDTH + 2 * SSD_GROUPS * SSD_STATE
MIX_WIDTH = GLA_VAL + SSD_WIDTH
SPLIT_SIZES = (GLA_KEY, GLA_KEY, GLA_VAL, GLA_VAL, N_DIR * GLA_RANK,
               SSD_WIDTH, SSD_XBC, N_DIR * SSD_HEADS)
IN_WIDTH = (2 * GLA_KEY + 2 * GLA_VAL + N_DIR * GLA_RANK
            + SSD_WIDTH + SSD_XBC + N_DIR * SSD_HEADS)

kernel_name = "hybrid_gla_ssd_parallel_encoder"


def _rmsnorm(x, g):
    xf = x.astype(jnp.float32)
    r = lax.rsqrt(jnp.mean(xf * xf, axis=-1, keepdims=True) + EPS)
    return (xf * r * g.astype(jnp.float32)).astype(x.dtype)


def _flip(a):
    return jnp.flip(a, axis=1)


def _chunks(a, nc):
    return jnp.moveaxis(a.reshape((a.shape[0], nc, CHUNK) + a.shape[2:]), 1, 0)


def _unchunks(a):
    a = jnp.moveaxis(a, 0, 1)
    return a.reshape((a.shape[0], a.shape[1] * a.shape[2]) + a.shape[3:])


def _chunk_mask(strict):
    return jnp.tril(jnp.ones((CHUNK, CHUNK), dtype=bool), k=-1 if strict else 0)


def _gla_direction(q, k, v, la, strict):
    bt, L = q.shape[:2]
    nc = L // CHUNK
    mask = _chunk_mask(strict)[None, :, :, None, None]

    def step(S, inp):
        qc, kc, vc, lc = inp
        cum = jnp.cumsum(lc, axis=1)
        seg = jnp.where(mask, cum[:, :, None] - cum[:, None], -jnp.inf)
        att = jnp.einsum('bihk,bjhk,bijhk->bhij', qc, kc, jnp.exp(seg))
        y = jnp.einsum('bhij,bjhv->bihv', att, vc)
        y = y + jnp.einsum('bihk,bhkv->bihv', qc * jnp.exp(cum), S)
        tail = jnp.exp(cum[:, -1:] - cum)
        S = S * jnp.exp(cum[:, -1])[..., None] + jnp.einsum('bjhk,bjhv->bhkv', kc * tail, vc)
        return S, y

    S0 = jnp.zeros((bt, GLA_HEADS, GLA_DK, GLA_DV), jnp.float32)
    _, ys = lax.scan(step, S0, (_chunks(q, nc), _chunks(k, nc), _chunks(v, nc), _chunks(la, nc)))
    return _unchunks(ys)


def _ssd_direction(x, dt, A, bm, cm, strict):
    bt, L = x.shape[:2]
    nc = L // CHUNK
    mask = _chunk_mask(strict)[None, :, :, None, None]

    def step(hs, inp):
        xc, dtc, bc, cc = inp
        cum = jnp.cumsum(dtc * A, axis=1)
        seg = jnp.where(mask, cum[:, :, None] - cum[:, None], -jnp.inf)
        cb = jnp.einsum('bign,bjgn->bijg', cc, bc)
        w = cb[..., None] * jnp.exp(seg) * dtc[:, None]
        y = jnp.einsum('bijgr,bjgrp->bigrp', w, xc)
        y = y + jnp.einsum('bign,bgrpn->bigrp', cc, hs) * jnp.exp(cum)[..., None]
        tail = jnp.exp(cum[:, -1:] - cum) * dtc
        hs = hs * jnp.exp(cum[:, -1])[..., None, None] + jnp.einsum('bjgr,bjgn,bjgrp->bgrpn', tail, bc, xc)
        return hs, y

    h0 = jnp.zeros((bt, SSD_GROUPS, SSD_HPG, SSD_HEADDIM, SSD_STATE), jnp.float32)
    _, ys = lax.scan(step, h0, (_chunks(x, nc), _chunks(dt, nc), _chunks(bm, nc), _chunks(cm, nc)))
    return _unchunks(ys)


def _centred_dwconv(x, w):
    pad = (SSD_CONV - 1) // 2
    return lax.conv_general_dilated(
        x, w[:, None, :].astype(x.dtype), window_strides=(1,), padding=[(pad, pad)],
        dimension_numbers=('NWC', 'WIO', 'NWC'), feature_group_count=x.shape[-1])


def _layer(h, p, norm_g, w_in, w_gla_gate, b_gla_gate, gla_onorm_g, conv_w, conv_b,
           dt_bias, a_log, d_skip, ssd_norm_g, w_out, w_ple_gate, w_ple_proj, ple_norm_g):
    f32 = jnp.float32
    bt, L, _ = h.shape
    u = _rmsnorm(h, norm_g)
    proj = u @ w_in
    pieces = []
    off = 0
    for size in SPLIT_SIZES:
        pieces.append(proj[..., off:off + size])
        off += size
    q, k, v, z_gla, lr, z_ssd, xbc, dt_raw = pieces

    q = (q.astype(f32) * GLA_DK ** -0.5).reshape(bt, L, GLA_HEADS, GLA_DK)
    k = k.astype(f32).reshape(bt, L, GLA_HEADS, GLA_DK)
    v = v.astype(f32).reshape(bt, L, GLA_HEADS, GLA_DV)
    lr = lr.astype(f32).reshape(bt, L, N_DIR, GLA_RANK)
    gate_logits = jnp.einsum('bldr,drk->bldk', lr, w_gla_gate.astype(f32)) + b_gla_gate.astype(f32)
    la = (jax.nn.log_sigmoid(gate_logits) / GLA_TAU).reshape(bt, L, N_DIR, GLA_HEADS, GLA_DK)
    o = (_gla_direction(q, k, v, la[:, :, 0], False)
         + _flip(_gla_direction(_flip(q), _flip(k), _flip(v), _flip(la[:, :, 1]), True)))
    o = _rmsnorm(o, gla_onorm_g).reshape(bt, L, GLA_VAL)
    o = o * jax.nn.silu(z_gla.astype(f32))

    xbc = jax.nn.silu(_centred_dwconv(xbc, conv_w) + conv_b).astype(f32)
    gn = SSD_GROUPS * SSD_STATE
    xs = xbc[..., :SSD_WIDTH].reshape(bt, L, SSD_GROUPS, SSD_HPG, SSD_HEADDIM)
    bm = xbc[..., SSD_WIDTH:SSD_WIDTH + gn].reshape(bt, L, SSD_GROUPS, SSD_STATE)
    cm = xbc[..., SSD_WIDTH + gn:].reshape(bt, L, SSD_GROUPS, SSD_STATE)
    dt = jax.nn.softplus(dt_raw.astype(f32).reshape(bt, L, N_DIR, SSD_HEADS) + dt_bias.astype(f32))
    dt = dt.reshape(bt, L, N_DIR, SSD_GROUPS, SSD_HPG)
    A = -jnp.exp(a_log.astype(f32)).reshape(N_DIR, SSD_GROUPS, SSD_HPG)
    y = (_ssd_direction(xs, dt[:, :, 0], A[0], bm, cm, False)
         + _flip(_ssd_direction(_flip(xs), _flip(dt[:, :, 1]), A[1], _flip(bm), _flip(cm), True)))
    y = y + d_skip.astype(f32).reshape(SSD_GROUPS, SSD_HPG)[..., None] * xs
    y = y.reshape(bt, L, SSD_WIDTH) * jax.nn.silu(z_ssd.astype(f32))
    y = _rmsnorm(y, ssd_norm_g)

    mix = jnp.concatenate([o, y], axis=-1).astype(h.dtype)
    h = h + mix @ w_out

    e = _rmsnorm(p @ w_ple_proj, ple_norm_g)
    h = h + jax.nn.sigmoid(h @ w_ple_gate) * e
    return h


def setup_inputs(seed: int = 0) -> dict:
    key = jax.random.key(seed)
    ks = jax.random.split(key, 24)
    nrm = jax.random.normal
    f32 = jnp.float32
    dt0 = jnp.exp(jax.random.uniform(ks[12], (DEPTH, N_DIR, SSD_HEADS), f32,
                                     float(np.log(1e-3)), float(np.log(1e-1))))
    return {
        "x_prompt": nrm(ks[0], (BATCH, SEQ, D_MODEL), f32),
        "x_sample": nrm(ks[1], (DEC_BATCH, DEC_SEQ, D_MODEL), f32),
        "p_prompt": nrm(ks[2], (DEPTH, BATCH, SEQ, D_PLE), f32),
        "p_sample": nrm(ks[3], (DEPTH, DEC_BATCH, DEC_SEQ, D_PLE), f32),
        "norm_g": 1.0 + 0.02 * nrm(ks[4], (DEPTH, D_MODEL), f32),
        "w_in": nrm(ks[5], (DEPTH, D_MODEL, IN_WIDTH), f32) * D_MODEL ** -0.5,
        "w_gla_gate": nrm(ks[6], (DEPTH, N_DIR, GLA_RANK, GLA_KEY), f32) * GLA_RANK ** -0.5,
        "b_gla_gate": 0.1 * nrm(ks[7], (DEPTH, N_DIR, GLA_KEY), f32),
        "gla_onorm_g": 1.0 + 0.02 * nrm(ks[8], (DEPTH, GLA_DV), f32),
        "conv_w": nrm(ks[9], (DEPTH, SSD_CONV, SSD_XBC), f32) * SSD_CONV ** -0.5,
        "conv_b": 0.02 * nrm(ks[10], (DEPTH, SSD_XBC), f32),
        "dt_bias": dt0 + jnp.log(-jnp.expm1(-dt0)),
        "a_log": jnp.log(jax.random.uniform(ks[11], (DEPTH, N_DIR, SSD_HEADS), f32, 1.0, 16.0)),
        "d_skip": 1.0 + 0.1 * nrm(ks[13], (DEPTH, SSD_HEADS), f32),
        "ssd_norm_g": 1.0 + 0.02 * nrm(ks[14], (DEPTH, SSD_WIDTH), f32),
        "w_out": nrm(ks[15], (DEPTH, MIX_WIDTH, D_MODEL), f32) * MIX_WIDTH ** -0.5,
        "w_ple_gate": nrm(ks[16], (DEPTH, D_MODEL, D_MODEL), f32) * D_MODEL ** -0.5,
        "w_ple_proj": nrm(ks[17], (DEPTH, D_PLE, D_MODEL), f32) * D_PLE ** -0.5,
        "ple_norm_g": 1.0 + 0.02 * nrm(ks[18], (DEPTH, D_MODEL), f32),
        "final_norm_g": 1.0 + 0.02 * nrm(ks[19], (D_MODEL,), f32),
    }


def reference(x_prompt, x_sample, p_prompt, p_sample, norm_g, w_in, w_gla_gate, b_gla_gate,
              gla_onorm_g, conv_w, conv_b, dt_bias, a_log, d_skip, ssd_norm_g, w_out,
              w_ple_gate, w_ple_proj, ple_norm_g, final_norm_g):
    def trunk(h, p):
        for i in range(DEPTH):
            h = _layer(h, p[i], norm_g[i], w_in[i], w_gla_gate[i], b_gla_gate[i], gla_onorm_g[i],
                       conv_w[i], conv_b[i], dt_bias[i], a_log[i], d_skip[i], ssd_norm_g[i],
                       w_out[i], w_ple_gate[i], w_ple_proj[i], ple_norm_g[i])
        return _rmsnorm(h, final_norm_g)

    y_prompt = trunk(x_prompt, p_prompt)
    y_sample = trunk(x_sample, p_sample)
    return (y_prompt, y_sample)
```

```python
import functools

import numpy as np
import jax
import jax.numpy as jnp
from jax import lax
from jax.experimental import pallas as pl
from jax.experimental.pallas import tpu as pltpu

F32 = jnp.float32
BF16 = jnp.bfloat16

D_MODEL = 1024
D_PLE = 256
EPS = 1e-6
N_DIR = 2
GLA_HEADS = 4
GLA_KEY = 512
GLA_VAL = 1024
GLA_DK = 128
GLA_DV = 256
GLA_RANK = 16
GLA_TAU = 16.0
SSD_WIDTH = 1024
SSD_HEADDIM = 64
SSD_HEADS = 16
SSD_GROUPS = 4
SSD_HPG = 4
SSD_STATE = 128
SSD_CONV = 5
SSD_GN = SSD_GROUPS * SSD_STATE
SSD_XBC = SSD_WIDTH + 2 * SSD_GN
MIX_WIDTH = GLA_VAL + SSD_WIDTH

LANES = 128
CHUNK = 128
HALO = 16
VMEM_LIMIT = 56 * 1024 * 1024

COL_Q = 0
COL_K = COL_Q + GLA_KEY
COL_V = COL_K + GLA_KEY
COL_ZG = COL_V + GLA_VAL
COL_ZS = COL_ZG + GLA_VAL
COL_XBC = COL_ZS + SSD_WIDTH
COL_LR = COL_XBC + SSD_XBC
COL_DT = COL_LR + LANES
PROJ_WIDTH = COL_DT + LANES

AUX_CUM, AUX_DT, AUX_ECUM, AUX_DTAIL = 0, 32, 64, 96
SEL_COLB = 0
SEL_ECUM = SSD_HEADS * CHUNK
SEL_DTAIL = SEL_ECUM + SSD_WIDTH
SEL_WIDTH = SEL_DTAIL + SSD_WIDTH


def _sigmoid(x):
    return 0.5 * (jnp.tanh(0.5 * x) + 1.0)


def _silu(x):
    return x * _sigmoid(x)


def _log1pexp_neg_abs(x):
    return jnp.log(1.0 + jnp.exp(-jnp.abs(x)))


def _log_sigmoid(x):
    return jnp.minimum(x, 0.0) - _log1pexp_neg_abs(x)


def _softplus(x):
    return jnp.maximum(x, 0.0) + _log1pexp_neg_abs(x)


def _rms(x, g):
    ms = jnp.mean(x * x, axis=-1, keepdims=True)
    return x * lax.rsqrt(ms + EPS) * g


def _split2(x):
    hi = x.astype(BF16)
    lo = (x - hi.astype(F32)).astype(BF16)
    return hi, lo


def _dot(a, b):
    return jnp.dot(a, b, preferred_element_type=F32)


def _dot_nt(a, b):
    return lax.dot_general(a, b, (((1,), (1,)), ((), ())), preferred_element_type=F32)


def _dot_tn(a, b):
    return lax.dot_general(a, b, (((0,), (0,)), ((), ())), preferred_element_type=F32)


def _const_spec(shape):
    nd = len(shape)
    return pl.BlockSpec(shape, lambda *_: (0,) * nd, pipeline_mode=pl.Buffered(1))


def _proj_kernel(h_ref, ng_ref, w_ref, wg_ref, bg_ref, dtb_ref, alog_ref, tril_ref, triu_ref,
                 q_ref, k_ref, v_ref, sg_ref, sz_ref, xbc_ref, g_ref, aux_ref):
    tm = h_ref.shape[0]
    u = _rms(h_ref[...], ng_ref[...]).astype(BF16)

    def mm(a, b):
        return _dot(u, w_ref[:, a:b])

    q_ref[...] = (mm(COL_Q, COL_K) * (GLA_DK ** -0.5)).astype(BF16)
    k_ref[...] = mm(COL_K, COL_V).astype(BF16)
    v_ref[...] = mm(COL_V, COL_ZG).astype(BF16)
    sg_ref[...] = _silu(mm(COL_ZG, COL_ZS)).astype(BF16)
    sz_ref[...] = _silu(mm(COL_ZS, COL_XBC)).astype(BF16)
    xbc_ref[...] = mm(COL_XBC, COL_LR).astype(BF16)

    lr = mm(COL_LR, COL_DT).astype(BF16)
    la = _log_sigmoid(_dot(lr, wg_ref[...]) + bg_ref[...]) * (1.0 / GLA_TAU)

    dt = _softplus(mm(COL_DT, PROJ_WIDTH) + dtb_ref[...])
    da = dt * (-jnp.exp(alog_ref[...]))
    lane = lax.broadcasted_iota(jnp.int32, (1, LANES), 1)
    is_fwd = (lane % 32) < SSD_HEADS
    grp = lane // 32

    tril = tril_ref[...]
    triu = triu_ref[...]
    for c in range(tm // CHUNK):
        sl = slice(c * CHUNK, (c + 1) * CHUNK)
        hi, lo = _split2(la[sl])
        st = jnp.concatenate([hi, lo], axis=0)
        g_ref[sl, :GLA_KEY] = _dot(tril, st[:, :GLA_KEY])
        g_ref[sl, GLA_KEY:] = _dot(triu, st[:, GLA_KEY:])

        dhi, dlo = _split2(da[sl])
        dst = jnp.concatenate([dhi, dlo], axis=0)
        cum = jnp.where(is_fwd, _dot(tril, dst), _dot(triu, dst))
        tot = jnp.where(is_fwd, cum[CHUNK - 1:CHUNK], cum[0:1])
        dtc = dt[sl]
        aux = jnp.where(grp == 0, cum,
                        jnp.where(grp == 1, dtc,
                                  jnp.where(grp == 2, jnp.exp(cum), dtc * jnp.exp(tot - cum))))
        aux_ref[sl, :] = aux


def _proj_call(h, ng, w, wg, bg, dtb, alog, tril, triu, tm):
    t = h.shape[0]
    row = lambda width: pl.BlockSpec((tm, width), lambda i: (i, 0))
    out_shape = (
        jax.ShapeDtypeStruct((t, GLA_KEY), BF16), jax.ShapeDtypeStruct((t, GLA_KEY), BF16),
        jax.ShapeDtypeStruct((t, GLA_VAL), BF16), jax.ShapeDtypeStruct((t, GLA_VAL), BF16),
        jax.ShapeDtypeStruct((t, SSD_WIDTH), BF16), jax.ShapeDtypeStruct((t, SSD_XBC), BF16),
        jax.ShapeDtypeStruct((t, 2 * GLA_KEY), F32), jax.ShapeDtypeStruct((t, LANES), F32),
    )
    return pl.pallas_call(
        _proj_kernel,
        grid=(t // tm,),
        in_specs=[row(D_MODEL), _const_spec(ng.shape), _const_spec(w.shape), _const_spec(wg.shape),
                  _const_spec(bg.shape), _const_spec(dtb.shape), _const_spec(alog.shape),
                  _const_spec(tril.shape), _const_spec(triu.shape)],
        out_specs=(row(GLA_KEY), row(GLA_KEY), row(GLA_VAL), row(GLA_VAL), row(SSD_WIDTH),
                   row(SSD_XBC), row(2 * GLA_KEY), row(LANES)),
        out_shape=out_shape,
        name="proj",
        compiler_params=pltpu.CompilerParams(dimension_semantics=("parallel",),
                                             vmem_limit_bytes=VMEM_LIMIT),
    )(h, ng, w, wg, bg, dtb, alog, tril, triu)


def _conv_kernel(x_ref, prev_ref, next_ref, w_ref, b_ref, o_ref, buf_ref, *, tiles_per_seq):
    tm = x_ref.shape[0]
    pad = (SSD_CONV - 1) // 2
    i = pl.program_id(0) % tiles_per_seq
    prev = prev_ref[...].astype(F32)[HALO - 8:HALO]
    nxt = next_ref[...].astype(F32)[0:8]
    buf_ref[0:8, :] = jnp.where(i == 0, 0.0, prev)
    buf_ref[8:8 + tm, :] = x_ref[...].astype(F32)
    buf_ref[8 + tm:16 + tm, :] = jnp.where(i == tiles_per_seq - 1, 0.0, nxt)
    acc = b_ref[...] + w_ref[0:1, :] * buf_ref[8 - pad:8 - pad + tm, :]
    for tap in range(1, SSD_CONV):
        acc = acc + w_ref[tap:tap + 1, :] * buf_ref[8 - pad + tap:8 - pad + tap + tm, :]
    o_ref[...] = _silu(acc).astype(BF16)


def _conv_call(xbc, cw, cb, tm, seq_len):
    t = xbc.shape[0]
    hb = tm // HALO
    last = t // HALO - 1
    return pl.pallas_call(
        functools.partial(_conv_kernel, tiles_per_seq=seq_len // tm),
        grid=(t // tm,),
        in_specs=[pl.BlockSpec((tm, SSD_XBC), lambda i: (i, 0)),
                  pl.BlockSpec((HALO, SSD_XBC), lambda i: (jnp.maximum(i * hb - 1, 0), 0)),
                  pl.BlockSpec((HALO, SSD_XBC), lambda i: (jnp.minimum((i + 1) * hb, last), 0)),
                  _const_spec(cw.shape), _const_spec(cb.shape)],
        out_specs=pl.BlockSpec((tm, SSD_XBC), lambda i: (i, 0)),
        out_shape=jax.ShapeDtypeStruct((t, SSD_XBC), BF16),
        scratch_shapes=[pltpu.VMEM((tm + 16, SSD_XBC), F32)],
        name="conv",
        compiler_params=pltpu.CompilerParams(dimension_semantics=("parallel",),
                                             vmem_limit_bytes=VMEM_LIMIT),
    )(xbc, xbc, xbc, cw, cb)


def _pair_mask(reverse):
    row = lax.broadcasted_iota(jnp.int32, (CHUNK, CHUNK), 0)
    col = lax.broadcasted_iota(jnp.int32, (CHUNK, CHUNK), 1)
    return (col > row) if reverse else (col <= row)


def _gla_kernel(q_ref, k_ref, v_ref, g_ref, y_ref, s_ref, *, reverse):
    @pl.when(pl.program_id(1) == 0)
    def _():
        s_ref[...] = jnp.zeros_like(s_ref)

    mask = _pair_mask(reverse)
    half = CHUNK // 2
    mid = half if reverse else half - 1
    end = 0 if reverse else CHUNK - 1
    for h in range(GLA_HEADS):
        ks = slice(h * GLA_DK, (h + 1) * GLA_DK)
        vs = slice(h * GLA_DV, (h + 1) * GLA_DV)
        g = g_ref[:, ks]
        ref = g[mid:mid + 1]
        d = g - ref
        qt = q_ref[:, ks].astype(F32) * jnp.exp(d)
        kt = k_ref[:, ks].astype(F32) * jnp.exp(-d)
        att = jnp.where(mask, _dot_nt(qt.astype(BF16), kt.astype(BF16)), 0.0)
        vh = v_ref[:, vs]
        s = s_ref[h]
        qs = (qt * jnp.exp(ref)).astype(BF16)
        y = _dot(att.astype(BF16), vh) + _dot(qs, s.astype(BF16))
        y_ref[:, vs] = y.astype(BF16)
        gtot = g[end:end + 1]
        kk = (kt * jnp.exp(gtot - ref)).astype(BF16)
        ecol = jnp.broadcast_to(jnp.exp(gtot), (GLA_DK, GLA_DK)).T
        s_ref[h] = s * jnp.concatenate([ecol, ecol], axis=1) + _dot_tn(kk, vh)


def _scan_index(nc, reverse):
    if reverse:
        return lambda s, c: (s * nc + (nc - 1 - c), 0)
    return lambda s, c: (s * nc + c, 0)


def _gla_call(q, k, v, g, n_seq, seq_len, reverse):
    t = q.shape[0]
    nc = seq_len // CHUNK
    idx = _scan_index(nc, reverse)
    gcol = 1 if reverse else 0
    gidx = (lambda s, c: (s * nc + (nc - 1 - c), gcol)) if reverse else (lambda s, c: (s * nc + c, gcol))
    return pl.pallas_call(
        functools.partial(_gla_kernel, reverse=reverse),
        grid=(n_seq, nc),
        in_specs=[pl.BlockSpec((CHUNK, GLA_KEY), idx), pl.BlockSpec((CHUNK, GLA_KEY), idx),
                  pl.BlockSpec((CHUNK, GLA_VAL), idx), pl.BlockSpec((CHUNK, GLA_KEY), gidx)],
        out_specs=pl.BlockSpec((CHUNK, GLA_VAL), idx),
        out_shape=jax.ShapeDtypeStruct((t, GLA_VAL), BF16),
        scratch_shapes=[pltpu.VMEM((GLA_HEADS, GLA_DK, GLA_DV), F32)],
        name="gla_bwd" if reverse else "gla_fwd",
        compiler_params=pltpu.CompilerParams(dimension_semantics=("arbitrary", "arbitrary"),
                                             vmem_limit_bytes=VMEM_LIMIT),
    )(q, k, v, g)


def _ssd_kernel(xbc_ref, aux_ref, sel_ref, hmask_ref, y_ref, h_ref, *, reverse):
    @pl.when(pl.program_id(1) == 0)
    def _():
        h_ref[...] = jnp.zeros_like(h_ref)

    mask = _pair_mask(reverse)
    off = SSD_HEADS if reverse else 0
    end = 0 if reverse else CHUNK - 1
    hi, lo = _split2(aux_ref[...])
    hl = jnp.concatenate([hi, lo], axis=1)
    aux_t = (hi.astype(F32) + lo.astype(F32)).T
    ecum = _dot(hl, sel_ref[:, SEL_ECUM:SEL_DTAIL])
    dtail = _dot(hl, sel_ref[:, SEL_DTAIL:SEL_WIDTH])
    etot = ecum[end:end + 1]
    gw = SSD_HPG * SSD_HEADDIM
    for g in range(SSD_GROUPS):
        bg = xbc_ref[:, SSD_WIDTH + g * SSD_STATE:SSD_WIDTH + (g + 1) * SSD_STATE]
        cg = xbc_ref[:, SSD_WIDTH + SSD_GN + g * SSD_STATE:SSD_WIDTH + SSD_GN + (g + 1) * SSD_STATE]
        cb = jnp.where(mask, _dot_nt(cg, bg), 0.0)
        colb = _dot(hl, sel_ref[:, SEL_COLB + g * SSD_HPG * CHUNK:SEL_COLB + (g + 1) * SSD_HPG * CHUNK])
        ws = []
        for r in range(SSD_HPG):
            hd = g * SSD_HPG + r
            cum_i = colb[:, r * CHUNK:(r + 1) * CHUNK]
            cum_j = aux_t[AUX_CUM + off + hd:AUX_CUM + off + hd + 1, :]
            dt_j = aux_t[AUX_DT + off + hd:AUX_DT + off + hd + 1, :]
            decay = jnp.exp(jnp.minimum(cum_i - cum_j, 0.0))
            ws.append((cb * decay * dt_j).astype(BF16))
        gs = slice(g * gw, (g + 1) * gw)
        xg = xbc_ref[:, gs]
        xbd = jnp.concatenate([xg] * SSD_HPG, axis=0) * hmask_ref[...]
        hg = h_ref[g]
        y = _dot(jnp.concatenate(ws, axis=1), xbd) + _dot(cg, hg.astype(BF16)) * ecum[:, gs]
        y_ref[:, gs] = y.astype(BF16)
        xt = (xg.astype(F32) * dtail[:, gs]).astype(BF16)
        h_ref[g] = hg * etot[:, gs] + _dot_tn(bg, xt)


def _ssd_call(xbc, aux, sel, hmask, n_seq, seq_len, reverse):
    t = xbc.shape[0]
    nc = seq_len // CHUNK
    idx = _scan_index(nc, reverse)
    return pl.pallas_call(
        functools.partial(_ssd_kernel, reverse=reverse),
        grid=(n_seq, nc),
        in_specs=[pl.BlockSpec((CHUNK, SSD_XBC), idx), pl.BlockSpec((CHUNK, LANES), idx),
                  _const_spec(sel.shape), _const_spec(hmask.shape)],
        out_specs=pl.BlockSpec((CHUNK, SSD_WIDTH), idx),
        out_shape=jax.ShapeDtypeStruct((t, SSD_WIDTH), BF16),
        scratch_shapes=[pltpu.VMEM((SSD_GROUPS, SSD_STATE, SSD_HPG * SSD_HEADDIM), F32)],
        name="ssd_bwd" if reverse else "ssd_fwd",
        compiler_params=pltpu.CompilerParams(dimension_semantics=("arbitrary", "arbitrary"),
                                             vmem_limit_bytes=VMEM_LIMIT),
    )(xbc, aux, sel, hmask)


def _merge_kernel(h_ref, ygf_ref, ygb_ref, ysf_ref, ysb_ref, sg_ref, sz_ref, xs_ref, p_ref,
                  og_ref, dsk_ref, sng_ref, wo_ref, wpg_ref, wpp_ref, png_ref, fng_ref,
                  o_ref, *, final):
    o = ygf_ref[...].astype(F32) + ygb_ref[...].astype(F32)
    og = og_ref[...]
    heads = [_rms(o[:, h * GLA_DV:(h + 1) * GLA_DV], og) for h in range(GLA_HEADS)]
    o = jnp.concatenate(heads, axis=1) * sg_ref[...].astype(F32)
    y = ysf_ref[...].astype(F32) + ysb_ref[...].astype(F32) + dsk_ref[...] * xs_ref[...].astype(F32)
    y = _rms(y * sz_ref[...].astype(F32), sng_ref[...])
    h = h_ref[...] + _dot(o.astype(BF16), wo_ref[:GLA_VAL, :]) + _dot(y.astype(BF16), wo_ref[GLA_VAL:, :])
    e = _rms(_dot(p_ref[...].astype(BF16), wpp_ref[...]), png_ref[...])
    h = h + _sigmoid(_dot(h.astype(BF16), wpg_ref[...])) * e
    if final:
        h = _rms(h, fng_ref[...])
    o_ref[...] = h


def _merge_call(h, ygf, ygb, ysf, ysb, sg, sz, xbc, p, og, dsk, sng, wo, wpg, wpp, png, fng, tm, final):
    t = h.shape[0]
    row = lambda width: pl.BlockSpec((tm, width), lambda i: (i, 0))
    consts = (og, dsk, sng, wo, wpg, wpp, png, fng)
    return pl.pallas_call(
        functools.partial(_merge_kernel, final=final),
        grid=(t // tm,),
        in_specs=[row(D_MODEL)] + [row(GLA_VAL)] * 2 + [row(SSD_WIDTH)] * 2
                 + [row(GLA_VAL), row(SSD_WIDTH), row(SSD_WIDTH), row(D_PLE)]
                 + [_const_spec(c.shape) for c in consts],
        out_specs=row(D_MODEL),
        out_shape=jax.ShapeDtypeStruct((t, D_MODEL), F32),
        name="merge",
        compiler_params=pltpu.CompilerParams(dimension_semantics=("parallel",),
                                             vmem_limit_bytes=VMEM_LIMIT),
    )(h, ygf, ygb, ysf, ysb, sg, sz, xbc, p, *consts)


def _scan_constants():
    r = np.arange(CHUNK)
    tril = (r[None, :] <= r[:, None]).astype(np.float32)
    triu = (r[None, :] >= r[:, None]).astype(np.float32)
    tril2 = jnp.asarray(np.concatenate([tril, tril], axis=1), BF16)
    triu2 = jnp.asarray(np.concatenate([triu, triu], axis=1), BF16)
    sels = []
    for off in (0, SSD_HEADS):
        sel = np.zeros((LANES, SEL_WIDTH), np.float32)
        for hd in range(SSD_HEADS):
            sel[AUX_CUM + off + hd, SEL_COLB + hd * CHUNK:SEL_COLB + (hd + 1) * CHUNK] = 1.0
            sel[AUX_ECUM + off + hd, SEL_ECUM + hd * SSD_HEADDIM:SEL_ECUM + (hd + 1) * SSD_HEADDIM] = 1.0
            sel[AUX_DTAIL + off + hd, SEL_DTAIL + hd * SSD_HEADDIM:SEL_DTAIL + (hd + 1) * SSD_HEADDIM] = 1.0
        sels.append(jnp.asarray(np.concatenate([sel, sel], axis=0), BF16))
    gw = SSD_HPG * SSD_HEADDIM
    hmask = np.zeros((SSD_HPG * CHUNK, gw), np.float32)
    for rr in range(SSD_HPG):
        hmask[rr * CHUNK:(rr + 1) * CHUNK, rr * SSD_HEADDIM:(rr + 1) * SSD_HEADDIM] = 1.0
    return tril2, triu2, sels[0], sels[1], jnp.asarray(hmask, BF16)


def _layer_params(i, norm_g, w_in, w_gla_gate, b_gla_gate, gla_onorm_g, conv_w, conv_b, dt_bias,
                  a_log, d_skip, ssd_norm_g, w_out, w_ple_gate, w_ple_proj, ple_norm_g):
    w = w_in[i]
    o_lr = 2 * GLA_KEY + 2 * GLA_VAL
    o_zs = o_lr + N_DIR * GLA_RANK
    o_xbc = o_zs + SSD_WIDTH
    o_dt = o_xbc + SSD_XBC
    lr_slab = jnp.pad(w[:, o_lr:o_zs], ((0, 0), (0, LANES - N_DIR * GLA_RANK)))
    dt_slab = jnp.tile(w[:, o_dt:], (1, LANES // (N_DIR * SSD_HEADS)))
    w_all = jnp.concatenate([w[:, :o_lr], w[:, o_zs:o_dt], lr_slab, dt_slab], axis=1).astype(BF16)
    wg = jnp.zeros((LANES, N_DIR * GLA_KEY), F32)
    wg = wg.at[:GLA_RANK, :GLA_KEY].set(w_gla_gate[i, 0])
    wg = wg.at[GLA_RANK:2 * GLA_RANK, GLA_KEY:].set(w_gla_gate[i, 1])
    reps = LANES // (N_DIR * SSD_HEADS)
    return dict(
        ng=norm_g[i][None, :], w=w_all, wg=wg.astype(BF16), bg=b_gla_gate[i].reshape(1, -1),
        dtb=jnp.tile(dt_bias[i].reshape(1, -1), (1, reps)), alog=jnp.tile(a_log[i].reshape(1, -1), (1, reps)),
        cw=conv_w[i], cb=conv_b[i][None, :], og=gla_onorm_g[i][None, :],
        dsk=jnp.repeat(d_skip[i], SSD_HEADDIM)[None, :], sng=ssd_norm_g[i][None, :],
        wo=w_out[i].astype(BF16), wpg=w_ple_gate[i].astype(BF16), wpp=w_ple_proj[i].astype(BF16),
        png=ple_norm_g[i][None, :])


def _tile(seq_len, want):
    tm = min(want, seq_len)
    assert seq_len % tm == 0 and tm % CHUNK == 0
    return tm


def _trunk(x, p, layers, fng, consts):
    n_seq, seq_len, _ = x.shape
    tril2, triu2, sel_f, sel_b, hmask = consts
    h = x.reshape(n_seq * seq_len, D_MODEL)
    tm = _tile(seq_len, 512)
    depth = len(layers)
    for i, lp in enumerate(layers):
        q, k, v, sg, sz, xbc_raw, g, aux = _proj_call(
            h, lp["ng"], lp["w"], lp["wg"], lp["bg"], lp["dtb"], lp["alog"], tril2, triu2, tm)
        xbc = _conv_call(xbc_raw, lp["cw"], lp["cb"], tm, seq_len)
        ygf = _gla_call(q, k, v, g, n_seq, seq_len, False)
        ygb = _gla_call(q, k, v, g, n_seq, seq_len, True)
        ysf = _ssd_call(xbc, aux, sel_f, hmask, n_seq, seq_len, False)
        ysb = _ssd_call(xbc, aux, sel_b, hmask, n_seq, seq_len, True)
        h = _merge_call(h, ygf, ygb, ysf, ysb, sg, sz, xbc, p[i].reshape(n_seq * seq_len, D_PLE),
                        lp["og"], lp["dsk"], lp["sng"], lp["wo"], lp["wpg"], lp["wpp"], lp["png"],
                        fng, tm, final=(i == depth - 1))
    return h.reshape(n_seq, seq_len, D_MODEL)


def kernel(x_prompt, x_sample, p_prompt, p_sample, norm_g, w_in, w_gla_gate, b_gla_gate, gla_onorm_g,
           conv_w, conv_b, dt_bias, a_log, d_skip, ssd_norm_g, w_out, w_ple_gate, w_ple_proj,
           ple_norm_g, final_norm_g):
    depth = w_in.shape[0]
    layers = [_layer_params(i, norm_g, w_in, w_gla_gate, b_gla_gate, gla_onorm_g, conv_w, conv_b,
                            dt_bias, a_log, d_skip, ssd_norm_g, w_out, w_ple_gate, w_ple_proj,
                            ple_norm_g) for i in range(depth)]
    consts = _scan_constants()
    fng = final_norm_g[None, :]
    return (_trunk(x_prompt, p_prompt, layers, fng, consts),
            _trunk(x_sample, p_sample, layers, fng, consts))
```

```python
import functools

import numpy as np
import jax
import jax.numpy as jnp
from jax import lax
from jax.experimental import pallas as pl
from jax.experimental.pallas import tpu as pltpu

F32 = jnp.float32
BF16 = jnp.bfloat16

D_MODEL = 1024
D_PLE = 256
EPS = 1e-6
N_DIR = 2
GLA_HEADS = 4
GLA_KEY = 512
GLA_VAL = 1024
GLA_DK = 128
GLA_DV = 256
GLA_RANK = 16
GLA_TAU = 16.0
SSD_WIDTH = 1024
SSD_HEADDIM = 64
SSD_HEADS = 16
SSD_GROUPS = 4
SSD_HPG = 4
SSD_STATE = 128
SSD_CONV = 5
SSD_GN = SSD_GROUPS * SSD_STATE
SSD_XBC = SSD_WIDTH + 2 * SSD_GN
MIX_WIDTH = GLA_VAL + SSD_WIDTH

LANES = 128
CHUNK = 128
SCAN_CHUNKS_PER_STEP = 4
HALO = 16
VMEM_LIMIT = 56 * 1024 * 1024

COL_Q = 0
COL_K = COL_Q + GLA_KEY
COL_V = COL_K + GLA_KEY
COL_ZG = COL_V + GLA_VAL
COL_ZS = COL_ZG + GLA_VAL
COL_XBC = COL_ZS + SSD_WIDTH
PROJ_WIDTH = COL_XBC + SSD_XBC
SRC_LR = 2 * GLA_KEY + 2 * GLA_VAL
SRC_ZS = SRC_LR + N_DIR * GLA_RANK
SRC_DT = SRC_ZS + SSD_WIDTH + SSD_XBC

AUX_CUM, AUX_DT, AUX_ECUM, AUX_DTAIL = 0, 32, 64, 96
SEL_COLB = 0
SEL_ECUM = SSD_HEADS * CHUNK
SEL_DTAIL = SEL_ECUM + SSD_WIDTH
SEL_WIDTH = SEL_DTAIL + SSD_WIDTH


def _sigmoid(x):
    return 0.5 * (jnp.tanh(0.5 * x) + 1.0)


def _silu(x):
    return x * _sigmoid(x)


def _log1pexp_neg_abs(x):
    return jnp.log(1.0 + jnp.exp(-jnp.abs(x)))


def _log_sigmoid(x):
    return jnp.minimum(x, 0.0) - _log1pexp_neg_abs(x)


def _softplus(x):
    return jnp.maximum(x, 0.0) + _log1pexp_neg_abs(x)


def _rms(x, g):
    ms = jnp.mean(x * x, axis=-1, keepdims=True)
    return x * lax.rsqrt(ms + EPS) * g


def _split2(x):
    hi = x.astype(BF16)
    lo = (x - hi.astype(F32)).astype(BF16)
    return hi, lo


def _dot(a, b):
    return jnp.dot(a, b, preferred_element_type=F32)


def _dot_nt(a, b):
    return lax.dot_general(a, b, (((1,), (1,)), ((), ())), preferred_element_type=F32)


def _dot_tn(a, b):
    return lax.dot_general(a, b, (((0,), (0,)), ((), ())), preferred_element_type=F32)


def _const_spec(shape):
    nd = len(shape)
    return pl.BlockSpec(shape, lambda *_: (0,) * nd, pipeline_mode=pl.Buffered(1))


def _pack_kernel(w_ref, o_ref):
    w = w_ref[0]
    o_ref[0, :, :SRC_LR] = w[:, :SRC_LR].astype(BF16)
    o_ref[0, :, SRC_LR:] = w[:, SRC_ZS:SRC_DT].astype(BF16)


def _pack_call(w_in):
    depth, rows, width = w_in.shape
    tr = 256
    return pl.pallas_call(
        _pack_kernel,
        grid=(depth, rows // tr),
        in_specs=[pl.BlockSpec((1, tr, width), lambda l, r: (l, r, 0))],
        out_specs=pl.BlockSpec((1, tr, PROJ_WIDTH), lambda l, r: (l, r, 0)),
        out_shape=jax.ShapeDtypeStruct((depth, rows, PROJ_WIDTH), BF16),
        name="pack",
        compiler_params=pltpu.CompilerParams(dimension_semantics=("parallel", "parallel"),
                                             vmem_limit_bytes=VMEM_LIMIT),
    )(w_in)


def _proj_kernel(h_ref, ng_ref, w_ref, wsl_ref, wg_ref, bg_ref, dtb_ref, alog_ref, tril_ref, triu_ref,
                 q_ref, k_ref, v_ref, sg_ref, sz_ref, xbc_ref, g_ref, aux_ref):
    tm = h_ref.shape[0]
    u = _rms(h_ref[...], ng_ref[...]).astype(BF16)

    def mm(a, b):
        return _dot(u, w_ref[:, a:b])

    q_ref[...] = (mm(COL_Q, COL_K) * (GLA_DK ** -0.5)).astype(BF16)
    k_ref[...] = mm(COL_K, COL_V).astype(BF16)
    v_ref[...] = mm(COL_V, COL_ZG).astype(BF16)
    sg_ref[...] = _silu(mm(COL_ZG, COL_ZS)).astype(BF16)
    sz_ref[...] = _silu(mm(COL_ZS, COL_XBC)).astype(BF16)
    xbc_ref[...] = mm(COL_XBC, PROJ_WIDTH).astype(BF16)

    lr = _dot(u, wsl_ref[:, :LANES]).astype(BF16)
    la = _log_sigmoid(_dot(lr, wg_ref[...]) + bg_ref[...]) * (1.0 / GLA_TAU)

    dt = _softplus(_dot(u, wsl_ref[:, LANES:]) + dtb_ref[...])
    da = dt * (-jnp.exp(alog_ref[...]))
    lane = lax.broadcasted_iota(jnp.int32, (1, LANES), 1)
    is_fwd = (lane % 32) < SSD_HEADS
    grp = lane // 32

    tril = tril_ref[...]
    triu = triu_ref[...]
    for c in range(tm // CHUNK):
        sl = slice(c * CHUNK, (c + 1) * CHUNK)
        hi, lo = _split2(la[sl])
        st = jnp.concatenate([hi, lo], axis=0)
        g_ref[sl, :GLA_KEY] = _dot(tril, st[:, :GLA_KEY])
        g_ref[sl, GLA_KEY:] = _dot(triu, st[:, GLA_KEY:])

        dhi, dlo = _split2(da[sl])
        dst = jnp.concatenate([dhi, dlo], axis=0)
        cum = jnp.where(is_fwd, _dot(tril, dst), _dot(triu, dst))
        tot = jnp.where(is_fwd, cum[CHUNK - 1:CHUNK], cum[0:1])
        dtc = dt[sl]
        aux = jnp.where(grp == 0, cum,
                        jnp.where(grp == 1, dtc,
                                  jnp.where(grp == 2, jnp.exp(cum), dtc * jnp.exp(tot - cum))))
        aux_ref[sl, :] = aux


def _proj_call(h, ng, w, wsl, wg, bg, dtb, alog, tril, triu, tm):
    t = h.shape[0]
    row = lambda width: pl.BlockSpec((tm, width), lambda i: (i, 0))
    out_shape = (
        jax.ShapeDtypeStruct((t, GLA_KEY), BF16), jax.ShapeDtypeStruct((t, GLA_KEY), BF16),
        jax.ShapeDtypeStruct((t, GLA_VAL), BF16), jax.ShapeDtypeStruct((t, GLA_VAL), BF16),
        jax.ShapeDtypeStruct((t, SSD_WIDTH), BF16), jax.ShapeDtypeStruct((t, SSD_XBC), BF16),
        jax.ShapeDtypeStruct((t, 2 * GLA_KEY), F32), jax.ShapeDtypeStruct((t, LANES), F32),
    )
    return pl.pallas_call(
        _proj_kernel,
        grid=(t // tm,),
        in_specs=[row(D_MODEL), _const_spec(ng.shape), _const_spec(w.shape), _const_spec(wsl.shape),
                  _const_spec(wg.shape), _const_spec(bg.shape), _const_spec(dtb.shape),
                  _const_spec(alog.shape), _const_spec(tril.shape), _const_spec(triu.shape)],
        out_specs=(row(GLA_KEY), row(GLA_KEY), row(GLA_VAL), row(GLA_VAL), row(SSD_WIDTH),
                   row(SSD_XBC), row(2 * GLA_KEY), row(LANES)),
        out_shape=out_shape,
        name="proj",
        compiler_params=pltpu.CompilerParams(dimension_semantics=("parallel",),
                                             vmem_limit_bytes=VMEM_LIMIT),
    )(h, ng, w, wsl, wg, bg, dtb, alog, tril, triu)


def _conv_kernel(x_ref, prev_ref, next_ref, w_ref, b_ref, o_ref, buf_ref, *, tiles_per_seq):
    tm = x_ref.shape[0]
    pad = (SSD_CONV - 1) // 2
    i = pl.program_id(0) % tiles_per_seq
    prev = prev_ref[...].astype(F32)[HALO - 8:HALO]
    nxt = next_ref[...].astype(F32)[0:8]
    buf_ref[0:8, :] = jnp.where(i == 0, 0.0, prev)
    buf_ref[8:8 + tm, :] = x_ref[...].astype(F32)
    buf_ref[8 + tm:16 + tm, :] = jnp.where(i == tiles_per_seq - 1, 0.0, nxt)
    acc = b_ref[...] + w_ref[0:1, :] * buf_ref[8 - pad:8 - pad + tm, :]
    for tap in range(1, SSD_CONV):
        acc = acc + w_ref[tap:tap + 1, :] * buf_ref[8 - pad + tap:8 - pad + tap + tm, :]
    o_ref[...] = _silu(acc).astype(BF16)


def _conv_call(xbc, cw, cb, tm, seq_len):
    t = xbc.shape[0]
    hb = tm // HALO
    last = t // HALO - 1
    return pl.pallas_call(
        functools.partial(_conv_kernel, tiles_per_seq=seq_len // tm),
        grid=(t // tm,),
        in_specs=[pl.BlockSpec((tm, SSD_XBC), lambda i: (i, 0)),
                  pl.BlockSpec((HALO, SSD_XBC), lambda i: (jnp.maximum(i * hb - 1, 0), 0)),
                  pl.BlockSpec((HALO, SSD_XBC), lambda i: (jnp.minimum((i + 1) * hb, last), 0)),
                  _const_spec(cw.shape), _const_spec(cb.shape)],
        out_specs=pl.BlockSpec((tm, SSD_XBC), lambda i: (i, 0)),
        out_shape=jax.ShapeDtypeStruct((t, SSD_XBC), BF16),
        scratch_shapes=[pltpu.VMEM((tm + 16, SSD_XBC), F32)],
        name="conv",
        compiler_params=pltpu.CompilerParams(dimension_semantics=("parallel",),
                                             vmem_limit_bytes=VMEM_LIMIT),
    )(xbc, xbc, xbc, cw, cb)


def _pair_mask(reverse):
    row = lax.broadcasted_iota(jnp.int32, (CHUNK, CHUNK), 0)
    col = lax.broadcasted_iota(jnp.int32, (CHUNK, CHUNK), 1)
    return (col > row) if reverse else (col <= row)


def _gla_chunk(q_ref, k_ref, v_ref, g_ref, y_ref, s_ref, c, reverse):
    rows = slice(c * CHUNK, (c + 1) * CHUNK)
    mask = _pair_mask(reverse)
    half = CHUNK // 2
    mid = half if reverse else half - 1
    end = 0 if reverse else CHUNK - 1
    for h in range(GLA_HEADS):
        ks = slice(h * GLA_DK, (h + 1) * GLA_DK)
        vs = slice(h * GLA_DV, (h + 1) * GLA_DV)
        g = g_ref[rows, ks]
        ref = g[mid:mid + 1]
        d = g - ref
        qt = q_ref[rows, ks].astype(F32) * jnp.exp(d)
        kt = k_ref[rows, ks].astype(F32) * jnp.exp(-d)
        att = jnp.where(mask, _dot_nt(qt.astype(BF16), kt.astype(BF16)), 0.0)
        vh = v_ref[rows, vs]
        s = s_ref[h]
        qs = (qt * jnp.exp(ref)).astype(BF16)
        y = _dot(jnp.concatenate([att.astype(BF16), qs], axis=1),
                 jnp.concatenate([vh, s.astype(BF16)], axis=0))
        y_ref[rows, vs] = y.astype(BF16)
        gtot = g[end:end + 1]
        kk = (kt * jnp.exp(gtot - ref)).astype(BF16)
        ecol = jnp.broadcast_to(jnp.exp(gtot), (GLA_DK, GLA_DK)).T
        s_ref[h] = s * jnp.concatenate([ecol, ecol], axis=1) + _dot_tn(kk, vh)


def _gla_kernel(qf_ref, kf_ref, vf_ref, gf_ref, qb_ref, kb_ref, vb_ref, gb_ref,
                yf_ref, yb_ref, sf_ref, sb_ref, *, cps):
    @pl.when(pl.program_id(1) == 0)
    def _():
        sf_ref[...] = jnp.zeros_like(sf_ref)
        sb_ref[...] = jnp.zeros_like(sb_ref)

    for c in range(cps):
        _gla_chunk(qf_ref, kf_ref, vf_ref, gf_ref, yf_ref, sf_ref, c, False)
        _gla_chunk(qb_ref, kb_ref, vb_ref, gb_ref, yb_ref, sb_ref, cps - 1 - c, True)


def _scan_specs(width, nb, rows, col=0):
    fwd = pl.BlockSpec((rows, width), lambda s, j: (s * nb + j, col))
    bwd = pl.BlockSpec((rows, width), lambda s, j: (s * nb + (nb - 1 - j), col))
    return fwd, bwd


def _gla_call(q, k, v, g, n_seq, seq_len, cps):
    t = q.shape[0]
    rows = cps * CHUNK
    nb = seq_len // rows
    kf, kb = _scan_specs(GLA_KEY, nb, rows)
    vf, vb = _scan_specs(GLA_VAL, nb, rows)
    gf, _ = _scan_specs(GLA_KEY, nb, rows, 0)
    _, gb = _scan_specs(GLA_KEY, nb, rows, 1)
    state = pltpu.VMEM((GLA_HEADS, GLA_DK, GLA_DV), F32)
    return pl.pallas_call(
        functools.partial(_gla_kernel, cps=cps),
        grid=(n_seq, nb),
        in_specs=[kf, kf, vf, gf, kb, kb, vb, gb],
        out_specs=(vf, vb),
        out_shape=(jax.ShapeDtypeStruct((t, GLA_VAL), BF16),) * 2,
        scratch_shapes=[state, state],
        name="gla",
        compiler_params=pltpu.CompilerParams(dimension_semantics=("arbitrary", "arbitrary"),
                                             vmem_limit_bytes=VMEM_LIMIT),
    )(q, k, v, g, q, k, v, g)


def _ssd_chunk(xbc_ref, aux_ref, sel_ref, hmask_ref, y_ref, h_ref, c, reverse):
    rows = slice(c * CHUNK, (c + 1) * CHUNK)
    mask = _pair_mask(reverse)
    off = SSD_HEADS if reverse else 0
    end = 0 if reverse else CHUNK - 1
    hi, lo = _split2(aux_ref[rows, :])
    hl = jnp.concatenate([hi, lo], axis=1)
    aux_t = (hi.astype(F32) + lo.astype(F32)).T
    ecum = _dot(hl, sel_ref[:, SEL_ECUM:SEL_DTAIL])
    dtail = _dot(hl, sel_ref[:, SEL_DTAIL:SEL_WIDTH])
    etot = ecum[end:end + 1]
    gw = SSD_HPG * SSD_HEADDIM
    for g in range(SSD_GROUPS):
        bg = xbc_ref[rows, SSD_WIDTH + g * SSD_STATE:SSD_WIDTH + (g + 1) * SSD_STATE]
        cg = xbc_ref[rows, SSD_WIDTH + SSD_GN + g * SSD_STATE:SSD_WIDTH + SSD_GN + (g + 1) * SSD_STATE]
        cb = jnp.where(mask, _dot_nt(cg, bg), 0.0)
        colb = _dot(hl, sel_ref[:, SEL_COLB + g * SSD_HPG * CHUNK:SEL_COLB + (g + 1) * SSD_HPG * CHUNK])
        ws = []
        for r in range(SSD_HPG):
            hd = g * SSD_HPG + r
            cum_i = colb[:, r * CHUNK:(r + 1) * CHUNK]
            cum_j = aux_t[AUX_CUM + off + hd:AUX_CUM + off + hd + 1, :]
            dt_j = aux_t[AUX_DT + off + hd:AUX_DT + off + hd + 1, :]
            decay = jnp.exp(jnp.minimum(cum_i - cum_j, 0.0))
            ws.append((cb * decay * dt_j).astype(BF16))
        gs = slice(g * gw, (g + 1) * gw)
        xg = xbc_ref[rows, gs]
        xbd = jnp.concatenate([xg] * SSD_HPG, axis=0) * hmask_ref[...]
        hg = h_ref[g]
        y = _dot(jnp.concatenate(ws, axis=1), xbd) + _dot(cg, hg.astype(BF16)) * ecum[:, gs]
        y_ref[rows, gs] = y.astype(BF16)
        xt = (xg.astype(F32) * dtail[:, gs]).astype(BF16)
        h_ref[g] = hg * etot[:, gs] + _dot_tn(bg, xt)


def _ssd_kernel(xf_ref, af_ref, xb_ref, ab_ref, self_ref, selb_ref, hmask_ref,
                yf_ref, yb_ref, hf_ref, hb_ref, *, cps):
    @pl.when(pl.program_id(1) == 0)
    def _():
        hf_ref[...] = jnp.zeros_like(hf_ref)
        hb_ref[...] = jnp.zeros_like(hb_ref)

    for c in range(cps):
        _ssd_chunk(xf_ref, af_ref, self_ref, hmask_ref, yf_ref, hf_ref, c, False)
        _ssd_chunk(xb_ref, ab_ref, selb_ref, hmask_ref, yb_ref, hb_ref, cps - 1 - c, True)


def _ssd_call(xbc, aux, sel_f, sel_b, hmask, n_seq, seq_len, cps):
    t = xbc.shape[0]
    rows = cps * CHUNK
    nb = seq_len // rows
    xf, xb = _scan_specs(SSD_XBC, nb, rows)
    af, ab = _scan_specs(LANES, nb, rows)
    yf, yb = _scan_specs(SSD_WIDTH, nb, rows)
    state = pltpu.VMEM((SSD_GROUPS, SSD_STATE, SSD_HPG * SSD_HEADDIM), F32)
    return pl.pallas_call(
        functools.partial(_ssd_kernel, cps=cps),
        grid=(n_seq, nb),
        in_specs=[xf, af, xb, ab, _const_spec(sel_f.shape), _const_spec(sel_b.shape),
                  _const_spec(hmask.shape)],
        out_specs=(yf, yb),
        out_shape=(jax.ShapeDtypeStruct((t, SSD_WIDTH), BF16),) * 2,
        scratch_shapes=[state, state],
        name="ssd",
        compiler_params=pltpu.CompilerParams(dimension_semantics=("arbitrary", "arbitrary"),
                                             vmem_limit_bytes=VMEM_LIMIT),
    )(xbc, aux, xbc, aux, sel_f, sel_b, hmask)


def _merge_kernel(h_ref, ygf_ref, ygb_ref, ysf_ref, ysb_ref, sg_ref, sz_ref, xs_ref, p_ref,
                  og_ref, dsk_ref, sng_ref, wo_ref, wpg_ref, wpp_ref, png_ref, fng_ref,
                  o_ref, *, final):
    o = ygf_ref[...].astype(F32) + ygb_ref[...].astype(F32)
    og = og_ref[...]
    heads = [_rms(o[:, h * GLA_DV:(h + 1) * GLA_DV], og) for h in range(GLA_HEADS)]
    o = jnp.concatenate(heads, axis=1) * sg_ref[...].astype(F32)
    y = ysf_ref[...].astype(F32) + ysb_ref[...].astype(F32) + dsk_ref[...] * xs_ref[...].astype(F32)
    y = _rms(y * sz_ref[...].astype(F32), sng_ref[...])
    h = h_ref[...] + _dot(o.astype(BF16), wo_ref[:GLA_VAL, :]) + _dot(y.astype(BF16), wo_ref[GLA_VAL:, :])
    e = _rms(_dot(p_ref[...].astype(BF16), wpp_ref[...]), png_ref[...])
    h = h + _sigmoid(_dot(h.astype(BF16), wpg_ref[...])) * e
    if final:
        h = _rms(h, fng_ref[...])
    o_ref[...] = h


def _merge_call(h, ygf, ygb, ysf, ysb, sg, sz, xbc, p, og, dsk, sng, wo, wpg, wpp, png, fng, tm, final):
    t = h.shape[0]
    row = lambda width: pl.BlockSpec((tm, width), lambda i: (i, 0))
    consts = (og, dsk, sng, wo, wpg, wpp, png, fng)
    return pl.pallas_call(
        functools.partial(_merge_kernel, final=final),
        grid=(t // tm,),
        in_specs=[row(D_MODEL)] + [row(GLA_VAL)] * 2 + [row(SSD_WIDTH)] * 2
                 + [row(GLA_VAL), row(SSD_WIDTH), row(SSD_WIDTH), row(D_PLE)]
                 + [_const_spec(c.shape) for c in consts],
        out_specs=row(D_MODEL),
        out_shape=jax.ShapeDtypeStruct((t, D_MODEL), F32),
        name="merge",
        compiler_params=pltpu.CompilerParams(dimension_semantics=("parallel",),
                                             vmem_limit_bytes=VMEM_LIMIT),
    )(h, ygf, ygb, ysf, ysb, sg, sz, xbc, p, *consts)


def _scan_constants():
    r = np.arange(CHUNK)
    tril = (r[None, :] <= r[:, None]).astype(np.float32)
    triu = (r[None, :] >= r[:, None]).astype(np.float32)
    tril2 = jnp.asarray(np.concatenate([tril, tril], axis=1), BF16)
    triu2 = jnp.asarray(np.concatenate([triu, triu], axis=1), BF16)
    sels = []
    for off in (0, SSD_HEADS):
        sel = np.zeros((LANES, SEL_WIDTH), np.float32)
        for hd in range(SSD_HEADS):
            sel[AUX_CUM + off + hd, SEL_COLB + hd * CHUNK:SEL_COLB + (hd + 1) * CHUNK] = 1.0
            sel[AUX_ECUM + off + hd, SEL_ECUM + hd * SSD_HEADDIM:SEL_ECUM + (hd + 1) * SSD_HEADDIM] = 1.0
            sel[AUX_DTAIL + off + hd, SEL_DTAIL + hd * SSD_HEADDIM:SEL_DTAIL + (hd + 1) * SSD_HEADDIM] = 1.0
        sels.append(jnp.asarray(np.concatenate([sel, sel], axis=0), BF16))
    gw = SSD_HPG * SSD_HEADDIM
    hmask = np.zeros((SSD_HPG * CHUNK, gw), np.float32)
    for rr in range(SSD_HPG):
        hmask[rr * CHUNK:(rr + 1) * CHUNK, rr * SSD_HEADDIM:(rr + 1) * SSD_HEADDIM] = 1.0
    return tril2, triu2, sels[0], sels[1], jnp.asarray(hmask, BF16)


def _layer_params(i, w_packed, norm_g, w_in, w_gla_gate, b_gla_gate, gla_onorm_g, conv_w, conv_b, dt_bias,
                  a_log, d_skip, ssd_norm_g, w_out, w_ple_gate, w_ple_proj, ple_norm_g):
    reps = LANES // (N_DIR * SSD_HEADS)
    lr_slab = jnp.pad(w_in[i, :, SRC_LR:SRC_ZS], ((0, 0), (0, LANES - N_DIR * GLA_RANK)))
    dt_slab = jnp.tile(w_in[i, :, SRC_DT:], (1, reps))
    wg = jnp.zeros((LANES, N_DIR * GLA_KEY), F32)
    wg = wg.at[:GLA_RANK, :GLA_KEY].set(w_gla_gate[i, 0])
    wg = wg.at[GLA_RANK:2 * GLA_RANK, GLA_KEY:].set(w_gla_gate[i, 1])
    return dict(
        ng=norm_g[i][None, :], w=w_packed[i], wsl=jnp.concatenate([lr_slab, dt_slab], axis=1).astype(BF16),
        wg=wg.astype(BF16), bg=b_gla_gate[i].reshape(1, -1),
        dtb=jnp.tile(dt_bias[i].reshape(1, -1), (1, reps)), alog=jnp.tile(a_log[i].reshape(1, -1), (1, reps)),
        cw=conv_w[i], cb=conv_b[i][None, :], og=gla_onorm_g[i][None, :],
        dsk=jnp.repeat(d_skip[i], SSD_HEADDIM)[None, :], sng=ssd_norm_g[i][None, :],
        wo=w_out[i].astype(BF16), wpg=w_ple_gate[i].astype(BF16), wpp=w_ple_proj[i].astype(BF16),
        png=ple_norm_g[i][None, :])


def _tile(seq_len, want):
    tm = min(want, seq_len)
    assert seq_len % tm == 0 and tm % CHUNK == 0
    return tm


def _trunk(x, p, layers, fng, consts):
    n_seq, seq_len, _ = x.shape
    tril2, triu2, sel_f, sel_b, hmask = consts
    h = x.reshape(n_seq * seq_len, D_MODEL)
    tm = _tile(seq_len, 512)
    cps = min(SCAN_CHUNKS_PER_STEP, seq_len // CHUNK)
    depth = len(layers)
    for i, lp in enumerate(layers):
        q, k, v, sg, sz, xbc_raw, g, aux = _proj_call(
            h, lp["ng"], lp["w"], lp["wsl"], lp["wg"], lp["bg"], lp["dtb"], lp["alog"], tril2, triu2, tm)
        xbc = _conv_call(xbc_raw, lp["cw"], lp["cb"], tm, seq_len)
        ygf, ygb = _gla_call(q, k, v, g, n_seq, seq_len, cps)
        ysf, ysb = _ssd_call(xbc, aux, sel_f, sel_b, hmask, n_seq, seq_len, cps)
        h = _merge_call(h, ygf, ygb, ysf, ysb, sg, sz, xbc, p[i].reshape(n_seq * seq_len, D_PLE),
                        lp["og"], lp["dsk"], lp["sng"], lp["wo"], lp["wpg"], lp["wpp"], lp["png"],
                        fng, tm, final=(i == depth - 1))
    return h.reshape(n_seq, seq_len, D_MODEL)


def kernel(x_prompt, x_sample, p_prompt, p_sample, norm_g, w_in, w_gla_gate, b_gla_gate, gla_onorm_g,
           conv_w, conv_b, dt_bias, a_log, d_skip, ssd_norm_g, w_out, w_ple_gate, w_ple_proj,
           ple_norm_g, final_norm_g):
    depth = w_in.shape[0]
    w_packed = _pack_call(w_in)
    layers = [_layer_params(i, w_packed, norm_g, w_in, w_gla_gate, b_gla_gate, gla_onorm_g, conv_w, conv_b,
                            dt_bias, a_log, d_skip, ssd_norm_g, w_out, w_ple_gate, w_ple_proj,
                            ple_norm_g) for i in range(depth)]
    consts = _scan_constants()
    fng = final_norm_g[None, :]
    return (_trunk(x_prompt, p_prompt, layers, fng, consts),
            _trunk(x_sample, p_sample, layers, fng, consts))
```

```python
import functools

import numpy as np
import jax
import jax.numpy as jnp
from jax import lax
from jax.experimental import pallas as pl
from jax.experimental.pallas import tpu as pltpu

F32 = jnp.float32
BF16 = jnp.bfloat16

D_MODEL = 1024
D_PLE = 256
EPS = 1e-6
N_DIR = 2
GLA_HEADS = 4
GLA_KEY = 512
GLA_VAL = 1024
GLA_DK = 128
GLA_DV = 256
GLA_RANK = 16
GLA_TAU = 16.0
SSD_WIDTH = 1024
SSD_HEADDIM = 64
SSD_HEADS = 16
SSD_GROUPS = 4
SSD_HPG = 4
SSD_STATE = 128
SSD_CONV = 5
SSD_GN = SSD_GROUPS * SSD_STATE
SSD_XBC = SSD_WIDTH + 2 * SSD_GN
MIX_WIDTH = GLA_VAL + SSD_WIDTH

LANES = 128
CHUNK = 128
SCAN_CHUNKS_PER_STEP = 4
CONV_SLAB = 256
CONV_ROWS = 64
VMEM_LIMIT = 56 * 1024 * 1024

COL_Q = 0
COL_K = COL_Q + GLA_KEY
COL_V = COL_K + GLA_KEY
COL_ZG = COL_V + GLA_VAL
COL_ZS = COL_ZG + GLA_VAL
COL_XBC = COL_ZS + SSD_WIDTH
PROJ_WIDTH = COL_XBC + SSD_XBC
SRC_LR = 2 * GLA_KEY + 2 * GLA_VAL
SRC_ZS = SRC_LR + N_DIR * GLA_RANK
SRC_DT = SRC_ZS + SSD_WIDTH + SSD_XBC

AUX_CUM, AUX_LDT, AUX_ECUM, AUX_DTAIL = 0, 32, 64, 96
LOG2E = 1.4426950408889634
MIN_LOG2 = -126.0
SEL_ECUM = 0
SEL_DTAIL = SEL_ECUM + SSD_WIDTH
SEL_WIDTH = SEL_DTAIL + SSD_WIDTH


def _sigmoid(x):
    return 0.5 * (jnp.tanh(0.5 * x) + 1.0)


def _silu(x):
    return x * _sigmoid(x)


def _log1pexp_neg_abs(x):
    return jnp.log(1.0 + jnp.exp(-jnp.abs(x)))


def _log_sigmoid(x):
    return jnp.minimum(x, 0.0) - _log1pexp_neg_abs(x)


def _softplus(x):
    return jnp.maximum(x, 0.0) + _log1pexp_neg_abs(x)


def _rms(x, g):
    ms = jnp.mean(x * x, axis=-1, keepdims=True)
    return x * lax.rsqrt(ms + EPS) * g


def _split2(x):
    hi = x.astype(BF16)
    lo = (x - hi.astype(F32)).astype(BF16)
    return hi, lo


def _dot(a, b):
    return jnp.dot(a, b, preferred_element_type=F32)


def _dot_nt(a, b):
    return lax.dot_general(a, b, (((1,), (1,)), ((), ())), preferred_element_type=F32)


def _dot_tn(a, b):
    return lax.dot_general(a, b, (((0,), (0,)), ((), ())), preferred_element_type=F32)


def _const_spec(shape):
    nd = len(shape)
    return pl.BlockSpec(shape, lambda *_: (0,) * nd, pipeline_mode=pl.Buffered(1))


def _pack_kernel(w_ref, o_ref):
    w = w_ref[0]
    o_ref[0, :, :SRC_LR] = w[:, :SRC_LR].astype(BF16)
    o_ref[0, :, SRC_LR:] = w[:, SRC_ZS:SRC_DT].astype(BF16)


def _pack_call(w_in):
    depth, rows, width = w_in.shape
    tr = 256
    return pl.pallas_call(
        _pack_kernel,
        grid=(depth, rows // tr),
        in_specs=[pl.BlockSpec((1, tr, width), lambda l, r: (l, r, 0))],
        out_specs=pl.BlockSpec((1, tr, PROJ_WIDTH), lambda l, r: (l, r, 0)),
        out_shape=jax.ShapeDtypeStruct((depth, rows, PROJ_WIDTH), BF16),
        name="pack",
        compiler_params=pltpu.CompilerParams(dimension_semantics=("parallel", "parallel"),
                                             vmem_limit_bytes=VMEM_LIMIT),
    )(w_in)


def _proj_kernel(h_ref, ng_ref, w_ref, wsl_ref, wg_ref, bg_ref, dtb_ref, alog_ref, tril_ref, triu_ref,
                 cw_ref, cb_ref, q_ref, k_ref, v_ref, sg_ref, sz_ref, xbc_ref, g_ref, aux_ref, bufa_ref, bufb_ref,
                 *, tiles_per_seq):
    i = pl.program_id(0)
    refs = (h_ref, ng_ref, w_ref, wsl_ref, wg_ref, bg_ref, dtb_ref, alog_ref, tril_ref, triu_ref,
            cw_ref, cb_ref, q_ref, k_ref, v_ref, sg_ref, sz_ref, xbc_ref, g_ref, aux_ref)
    seq_start = (i % tiles_per_seq) == 0

    @pl.when(i == 0)
    def _():
        bufb_ref[...] = jnp.zeros(bufb_ref.shape, F32)

    @pl.when(i % 2 == 0)
    def _():
        _proj_tile(*refs, bufa_ref, bufb_ref, seq_start)

    @pl.when(i % 2 == 1)
    def _():
        _proj_tile(*refs, bufb_ref, bufa_ref, seq_start)


def _proj_tile(h_ref, ng_ref, w_ref, wsl_ref, wg_ref, bg_ref, dtb_ref, alog_ref, tril_ref, triu_ref,
               cw_ref, cb_ref, q_ref, k_ref, v_ref, sg_ref, sz_ref, xbc_ref, g_ref, aux_ref,
               cur_ref, prv_ref, seq_start):
    tm = h_ref.shape[0]
    pad = (SSD_CONV - 1) // 2
    u = _rms(h_ref[...], ng_ref[...]).astype(BF16)

    def mm(a, b):
        return _dot(u, w_ref[:, a:b])

    slab = _dot(u, wsl_ref[...])
    lr = slab[:, :LANES].astype(BF16)
    la = _log_sigmoid(_dot(lr, wg_ref[...]) + bg_ref[...]) * (LOG2E / GLA_TAU)

    dt = _softplus(slab[:, LANES:] + dtb_ref[...])
    da = dt * (-LOG2E * jnp.exp(alog_ref[...]))
    lane = lax.broadcasted_iota(jnp.int32, (1, LANES), 1)
    is_fwd = (lane % 32) < SSD_HEADS
    grp = lane // 32

    for j in range(SSD_XBC // CONV_SLAB):
        cs = slice(j * CONV_SLAB, (j + 1) * CONV_SLAB)
        xr = mm(COL_XBC + j * CONV_SLAB, COL_XBC + (j + 1) * CONV_SLAB)
        cur_ref[8:8 + tm, cs] = xr
        cur_ref[0:8, cs] = jnp.where(seq_start, 0.0, prv_ref[tm:tm + 8, cs])
        prv_ref[8 + tm:16 + tm, cs] = jnp.where(seq_start, 0.0, xr[0:8])

    tril = tril_ref[...]
    triu = triu_ref[...]
    for c in range(tm // CHUNK):
        sl = slice(c * CHUNK, (c + 1) * CHUNK)
        hi, lo = _split2(la[sl])
        st = jnp.concatenate([hi, lo], axis=0)
        g_ref[sl, :GLA_KEY] = _dot(tril, st[:, :GLA_KEY])
        g_ref[sl, GLA_KEY:] = _dot(triu, st[:, GLA_KEY:])

        dhi, dlo = _split2(da[sl])
        dst = jnp.concatenate([dhi, dlo], axis=0)
        cum = jnp.where(is_fwd, _dot(tril, dst), _dot(triu, dst))
        tot = jnp.where(is_fwd, cum[CHUNK - 1:CHUNK], cum[0:1])
        dtc = dt[sl]
        aux = jnp.where(grp == 0, cum,
                        jnp.where(grp == 1, jnp.maximum(jnp.log2(dtc), MIN_LOG2),
                                  jnp.where(grp == 2, jnp.exp2(cum), dtc * jnp.exp2(tot - cum))))
        aux_ref[sl, :] = aux

    def conv_slab(j):
        cs = slice(j * CONV_SLAB, (j + 1) * CONV_SLAB)
        for r0 in range(0, tm, CONV_ROWS):
            acc = cb_ref[:, cs] + cw_ref[0:1, cs] * prv_ref[8 - pad + r0:8 - pad + r0 + CONV_ROWS, cs]
            for tap in range(1, SSD_CONV):
                lo_row = 8 - pad + tap + r0
                acc = acc + cw_ref[tap:tap + 1, cs] * prv_ref[lo_row:lo_row + CONV_ROWS, cs]
            xbc_ref[r0:r0 + CONV_ROWS, cs] = _silu(acc).astype(BF16)

    half = GLA_VAL // 2
    pieces = [(q_ref, 0, COL_Q, lambda r: r * (GLA_DK ** -0.5)), (k_ref, 0, COL_K, None),
              (v_ref, 0, COL_V, None), (v_ref, half, COL_V + half, None),
              (sg_ref, 0, COL_ZG, _silu), (sg_ref, half, COL_ZG + half, _silu),
              (sz_ref, 0, COL_ZS, _silu), (sz_ref, half, COL_ZS + half, _silu)]
    assert len(pieces) == SSD_XBC // CONV_SLAB and GLA_KEY == half
    for j, (ref, lo, col, post) in enumerate(pieces):
        r = mm(col, col + half)
        ref[:, lo:lo + half] = (post(r) if post else r).astype(BF16)
        conv_slab(j)


def _proj_call(h, ng, w, wsl, wg, bg, dtb, alog, tril, triu, cw, cb, tm, seq_len):
    t = h.shape[0]
    nb = t // tm
    row = lambda width: pl.BlockSpec((tm, width), lambda i: (jnp.minimum(i, nb - 1), 0))
    conv_row = pl.BlockSpec((tm, SSD_XBC), lambda i: (jnp.maximum(i - 1, 0), 0))
    consts = (ng, w, wsl, wg, bg, dtb, alog, tril, triu, cw, cb)
    out_shape = (
        jax.ShapeDtypeStruct((t, GLA_KEY), BF16), jax.ShapeDtypeStruct((t, GLA_KEY), BF16),
        jax.ShapeDtypeStruct((t, GLA_VAL), BF16), jax.ShapeDtypeStruct((t, GLA_VAL), BF16),
        jax.ShapeDtypeStruct((t, SSD_WIDTH), BF16), jax.ShapeDtypeStruct((t, SSD_XBC), BF16),
        jax.ShapeDtypeStruct((t, 2 * GLA_KEY), F32), jax.ShapeDtypeStruct((t, LANES), F32),
    )
    return pl.pallas_call(
        functools.partial(_proj_kernel, tiles_per_seq=seq_len // tm),
        grid=(nb + 1,),
        in_specs=[row(D_MODEL)] + [_const_spec(c.shape) for c in consts],
        out_specs=(row(GLA_KEY), row(GLA_KEY), row(GLA_VAL), row(GLA_VAL), row(SSD_WIDTH),
                   conv_row, row(2 * GLA_KEY), row(LANES)),
        out_shape=out_shape,
        scratch_shapes=[pltpu.VMEM((tm + 16, SSD_XBC), F32)] * 2,
        name="proj",
        compiler_params=pltpu.CompilerParams(dimension_semantics=("arbitrary",),
                                             vmem_limit_bytes=VMEM_LIMIT),
    )(h, *consts)


def _pair_mask(reverse):
    row = lax.broadcasted_iota(jnp.int32, (CHUNK, CHUNK), 0)
    col = lax.broadcasted_iota(jnp.int32, (CHUNK, CHUNK), 1)
    return (col > row) if reverse else (col <= row)


def _gla_chunks(jobs):
    half = CHUNK // 2
    work = []
    for q_ref, k_ref, v_ref, g_ref, y_ref, s_ref, c, reverse in jobs:
        rows = slice(c * CHUNK, (c + 1) * CHUNK)
        mid = half if reverse else half - 1
        end = 0 if reverse else CHUNK - 1
        for h in range(GLA_HEADS):
            ks = slice(h * GLA_DK, (h + 1) * GLA_DK)
            g = g_ref[rows, ks]
            ref = g[mid:mid + 1]
            gtot = g[end:end + 1]
            d = g - ref
            qt = q_ref[rows, ks].astype(F32) * jnp.exp2(d)
            kt = k_ref[rows, ks].astype(F32) * jnp.exp2(-d)
            att = _dot_nt(qt.astype(BF16), kt.astype(BF16))
            work.append(dict(att=att, qs=(qt * jnp.exp2(ref)).astype(BF16),
                             kk=(kt * jnp.exp2(gtot - ref)).astype(BF16), etot=jnp.exp2(gtot),
                             v_ref=v_ref, y_ref=y_ref, s_ref=s_ref, rows=rows, h=h, reverse=reverse))
    for w in work:
        vs = slice(w["h"] * GLA_DV, (w["h"] + 1) * GLA_DV)
        att = jnp.where(_pair_mask(w["reverse"]), w["att"], 0.0).astype(BF16)
        y = _dot(jnp.concatenate([att, w["qs"]], axis=1),
                 jnp.concatenate([w["v_ref"][w["rows"], vs], w["s_ref"][w["h"]].astype(BF16)], axis=0))
        w["y_ref"][w["rows"], vs] = y.astype(BF16)
    for w in work:
        vs = slice(w["h"] * GLA_DV, (w["h"] + 1) * GLA_DV)
        ecol = jnp.broadcast_to(w["etot"], (GLA_DK, GLA_DK)).T
        upd = _dot_tn(w["kk"], w["v_ref"][w["rows"], vs])
        w["s_ref"][w["h"]] = w["s_ref"][w["h"]] * jnp.concatenate([ecol, ecol], axis=1) + upd


def _gla_kernel(qf_ref, kf_ref, vf_ref, gf_ref, qb_ref, kb_ref, vb_ref, gb_ref,
                yf_ref, yb_ref, sf_ref, sb_ref, *, cps):
    @pl.when(pl.program_id(1) == 0)
    def _():
        sf_ref[...] = jnp.zeros_like(sf_ref)
        sb_ref[...] = jnp.zeros_like(sb_ref)

    for c in range(cps):
        _gla_chunks([(qf_ref, kf_ref, vf_ref, gf_ref, yf_ref, sf_ref, c, False),
                     (qb_ref, kb_ref, vb_ref, gb_ref, yb_ref, sb_ref, cps - 1 - c, True)])


def _scan_specs(width, nb, rows, col=0):
    fwd = pl.BlockSpec((rows, width), lambda s, j: (s * nb + j, col))
    bwd = pl.BlockSpec((rows, width), lambda s, j: (s * nb + (nb - 1 - j), col))
    return fwd, bwd


def _gla_call(q, k, v, g, n_seq, seq_len, cps):
    t = q.shape[0]
    rows = cps * CHUNK
    nb = seq_len // rows
    kf, kb = _scan_specs(GLA_KEY, nb, rows)
    vf, vb = _scan_specs(GLA_VAL, nb, rows)
    gf, _ = _scan_specs(GLA_KEY, nb, rows, 0)
    _, gb = _scan_specs(GLA_KEY, nb, rows, 1)
    state = pltpu.VMEM((GLA_HEADS, GLA_DK, GLA_DV), F32)
    return pl.pallas_call(
        functools.partial(_gla_kernel, cps=cps),
        grid=(n_seq, nb),
        in_specs=[kf, kf, vf, gf, kb, kb, vb, gb],
        out_specs=(vf, vb),
        out_shape=(jax.ShapeDtypeStruct((t, GLA_VAL), BF16),) * 2,
        scratch_shapes=[state, state],
        name="gla",
        compiler_params=pltpu.CompilerParams(dimension_semantics=("arbitrary", "arbitrary"),
                                             vmem_limit_bytes=VMEM_LIMIT),
    )(q, k, v, g, q, k, v, g)


def _ssd_chunks(jobs, hmask_ref):
    gw = SSD_HPG * SSD_HEADDIM
    work = []
    for xbc_ref, aux_ref, sel_ref, y_ref, h_ref, c, reverse in jobs:
        rows = slice(c * CHUNK, (c + 1) * CHUNK)
        off = SSD_HEADS if reverse else 0
        hi, lo = _split2(aux_ref[rows, :])
        hl = jnp.concatenate([hi, lo], axis=1)
        aux_r = hi.astype(F32) + lo.astype(F32)
        aux_t = aux_r.T
        ldt_j = aux_t[AUX_LDT + off:AUX_LDT + off + SSD_HEADS]
        rel_j = ldt_j - aux_t[AUX_CUM + off:AUX_CUM + off + SSD_HEADS]
        for g in range(SSD_GROUPS):
            bg = xbc_ref[rows, SSD_WIDTH + g * SSD_STATE:SSD_WIDTH + (g + 1) * SSD_STATE]
            cg = xbc_ref[rows, SSD_WIDTH + SSD_GN + g * SSD_STATE:SSD_WIDTH + SSD_GN + (g + 1) * SSD_STATE]
            work.append(dict(cb=_dot_nt(cg, bg), bg=bg, cg=cg, g=g, rows=rows, reverse=reverse, off=off,
                             hl=hl, aux_r=aux_r, ldt_j=ldt_j, rel_j=rel_j, sel_ref=sel_ref,
                             xbc_ref=xbc_ref, y_ref=y_ref, h_ref=h_ref))
    def expand(w):
        lo_col = w["g"] * gw
        return (_dot(w["hl"], w["sel_ref"][:, SEL_ECUM + lo_col:SEL_ECUM + lo_col + gw]),
                _dot(w["hl"], w["sel_ref"][:, SEL_DTAIL + lo_col:SEL_DTAIL + lo_col + gw]))

    expanded = {0: expand(work[0])}
    for idx, w in enumerate(work):
        if idx + 1 < len(work):
            expanded[idx + 1] = expand(work[idx + 1])
        ecum, dtail = expanded.pop(idx)
        g, off = w["g"], w["off"]
        gs = slice(g * gw, (g + 1) * gw)
        end = 0 if w["reverse"] else CHUNK - 1
        cb = jnp.where(_pair_mask(w["reverse"]), w["cb"], 0.0)
        ws = []
        for r in range(SSD_HPG):
            hd = g * SSD_HPG + r
            cum_i = jnp.broadcast_to(w["aux_r"][:, AUX_CUM + off + hd:AUX_CUM + off + hd + 1], (CHUNK, CHUNK))
            e = jnp.minimum(cum_i + w["rel_j"][hd:hd + 1], w["ldt_j"][hd:hd + 1])
            ws.append((cb * jnp.exp2(e)).astype(BF16))
        xg = w["xbc_ref"][w["rows"], gs]
        xbd = jnp.concatenate([xg] * SSD_HPG, axis=0) * hmask_ref[...]
        hg = w["h_ref"][g]
        y = _dot(jnp.concatenate(ws, axis=1), xbd) + _dot(w["cg"], hg.astype(BF16)) * ecum
        w["y_ref"][w["rows"], gs] = y.astype(BF16)
        xt = (xg.astype(F32) * dtail).astype(BF16)
        w["h_ref"][g] = hg * ecum[end:end + 1] + _dot_tn(w["bg"], xt)


def _ssd_kernel(xf_ref, af_ref, xb_ref, ab_ref, self_ref, selb_ref, hmask_ref,
                yf_ref, yb_ref, hf_ref, hb_ref, *, cps):
    @pl.when(pl.program_id(1) == 0)
    def _():
        hf_ref[...] = jnp.zeros_like(hf_ref)
        hb_ref[...] = jnp.zeros_like(hb_ref)

    for c in range(cps):
        _ssd_chunks([(xf_ref, af_ref, self_ref, yf_ref, hf_ref, c, False),
                     (xb_ref, ab_ref, selb_ref, yb_ref, hb_ref, cps - 1 - c, True)], hmask_ref)


def _ssd_call(xbc, aux, sel_f, sel_b, hmask, n_seq, seq_len, cps):
    t = xbc.shape[0]
    rows = cps * CHUNK
    nb = seq_len // rows
    xf, xb = _scan_specs(SSD_XBC, nb, rows)
    af, ab = _scan_specs(LANES, nb, rows)
    yf, yb = _scan_specs(SSD_WIDTH, nb, rows)
    state = pltpu.VMEM((SSD_GROUPS, SSD_STATE, SSD_HPG * SSD_HEADDIM), F32)
    return pl.pallas_call(
        functools.partial(_ssd_kernel, cps=cps),
        grid=(n_seq, nb),
        in_specs=[xf, af, xb, ab, _const_spec(sel_f.shape), _const_spec(sel_b.shape),
                  _const_spec(hmask.shape)],
        out_specs=(yf, yb),
        out_shape=(jax.ShapeDtypeStruct((t, SSD_WIDTH), BF16),) * 2,
        scratch_shapes=[state, state],
        name="ssd",
        compiler_params=pltpu.CompilerParams(dimension_semantics=("arbitrary", "arbitrary"),
                                             vmem_limit_bytes=VMEM_LIMIT),
    )(xbc, aux, xbc, aux, sel_f, sel_b, hmask)


def _merge_kernel(h_ref, ygf_ref, ygb_ref, ysf_ref, ysb_ref, sg_ref, sz_ref, xs_ref, p_ref,
                  og_ref, dsk_ref, sng_ref, wo_ref, wpg_ref, wpp_ref, png_ref, fng_ref,
                  o_ref, *, final):
    o = ygf_ref[...].astype(F32) + ygb_ref[...].astype(F32)
    og = og_ref[...]
    heads = [_rms(o[:, h * GLA_DV:(h + 1) * GLA_DV], og) for h in range(GLA_HEADS)]
    o = jnp.concatenate(heads, axis=1) * sg_ref[...].astype(F32)
    y = ysf_ref[...].astype(F32) + ysb_ref[...].astype(F32) + dsk_ref[...] * xs_ref[...].astype(F32)
    y = _rms(y * sz_ref[...].astype(F32), sng_ref[...])
    h = h_ref[...] + _dot(o.astype(BF16), wo_ref[:GLA_VAL, :]) + _dot(y.astype(BF16), wo_ref[GLA_VAL:, :])
    e = _rms(_dot(p_ref[...].astype(BF16), wpp_ref[...]), png_ref[...])
    h = h + _sigmoid(_dot(h.astype(BF16), wpg_ref[...])) * e
    if final:
        h = _rms(h, fng_ref[...])
    o_ref[...] = h


def _merge_call(h, ygf, ygb, ysf, ysb, sg, sz, xbc, p, og, dsk, sng, wo, wpg, wpp, png, fng, tm, final):
    t = h.shape[0]
    row = lambda width: pl.BlockSpec((tm, width), lambda i: (i, 0))
    consts = (og, dsk, sng, wo, wpg, wpp, png, fng)
    return pl.pallas_call(
        functools.partial(_merge_kernel, final=final),
        grid=(t // tm,),
        in_specs=[row(D_MODEL)] + [row(GLA_VAL)] * 2 + [row(SSD_WIDTH)] * 2
                 + [row(GLA_VAL), row(SSD_WIDTH), row(SSD_WIDTH), row(D_PLE)]
                 + [_const_spec(c.shape) for c in consts],
        out_specs=row(D_MODEL),
        out_shape=jax.ShapeDtypeStruct((t, D_MODEL), F32),
        name="merge",
        compiler_params=pltpu.CompilerParams(dimension_semantics=("parallel",),
                                             vmem_limit_bytes=VMEM_LIMIT),
    )(h, ygf, ygb, ysf, ysb, sg, sz, xbc, p, *consts)


def _scan_constants():
    r = np.arange(CHUNK)
    tril = (r[None, :] <= r[:, None]).astype(np.float32)
    triu = (r[None, :] >= r[:, None]).astype(np.float32)
    tril2 = jnp.asarray(np.concatenate([tril, tril], axis=1), BF16)
    triu2 = jnp.asarray(np.concatenate([triu, triu], axis=1), BF16)
    sels = []
    for off in (0, SSD_HEADS):
        sel = np.zeros((LANES, SEL_WIDTH), np.float32)
        for hd in range(SSD_HEADS):
            sel[AUX_ECUM + off + hd, SEL_ECUM + hd * SSD_HEADDIM:SEL_ECUM + (hd + 1) * SSD_HEADDIM] = 1.0
            sel[AUX_DTAIL + off + hd, SEL_DTAIL + hd * SSD_HEADDIM:SEL_DTAIL + (hd + 1) * SSD_HEADDIM] = 1.0
        sels.append(jnp.asarray(np.concatenate([sel, sel], axis=0), BF16))
    gw = SSD_HPG * SSD_HEADDIM
    hmask = np.zeros((SSD_HPG * CHUNK, gw), np.float32)
    for rr in range(SSD_HPG):
        hmask[rr * CHUNK:(rr + 1) * CHUNK, rr * SSD_HEADDIM:(rr + 1) * SSD_HEADDIM] = 1.0
    return tril2, triu2, sels[0], sels[1], jnp.asarray(hmask, BF16)


def _layer_params(i, w_packed, norm_g, w_in, w_gla_gate, b_gla_gate, gla_onorm_g, conv_w, conv_b, dt_bias,
                  a_log, d_skip, ssd_norm_g, w_out, w_ple_gate, w_ple_proj, ple_norm_g):
    reps = LANES // (N_DIR * SSD_HEADS)
    lr_slab = jnp.pad(w_in[i, :, SRC_LR:SRC_ZS], ((0, 0), (0, LANES - N_DIR * GLA_RANK)))
    dt_slab = jnp.tile(w_in[i, :, SRC_DT:], (1, reps))
    wg = jnp.zeros((LANES, N_DIR * GLA_KEY), F32)
    wg = wg.at[:GLA_RANK, :GLA_KEY].set(w_gla_gate[i, 0])
    wg = wg.at[GLA_RANK:2 * GLA_RANK, GLA_KEY:].set(w_gla_gate[i, 1])
    return dict(
        ng=norm_g[i][None, :], w=w_packed[i], wsl=jnp.concatenate([lr_slab, dt_slab], axis=1).astype(BF16),
        wg=wg.astype(BF16), bg=b_gla_gate[i].reshape(1, -1),
        dtb=jnp.tile(dt_bias[i].reshape(1, -1), (1, reps)), alog=jnp.tile(a_log[i].reshape(1, -1), (1, reps)),
        cw=conv_w[i], cb=conv_b[i][None, :], og=gla_onorm_g[i][None, :],
        dsk=jnp.repeat(d_skip[i], SSD_HEADDIM)[None, :], sng=ssd_norm_g[i][None, :],
        wo=w_out[i].astype(BF16), wpg=w_ple_gate[i].astype(BF16), wpp=w_ple_proj[i].astype(BF16),
        png=ple_norm_g[i][None, :])


def _tile(seq_len, want):
    tm = min(want, seq_len)
    assert seq_len % tm == 0 and tm % CHUNK == 0
    return tm


def _trunk(x, p, layers, fng, consts):
    n_seq, seq_len, _ = x.shape
    tril2, triu2, sel_f, sel_b, hmask = consts
    h = x.reshape(n_seq * seq_len, D_MODEL)
    tm = _tile(seq_len, 512)
    cps = min(SCAN_CHUNKS_PER_STEP, seq_len // CHUNK)
    depth = len(layers)
    for i, lp in enumerate(layers):
        q, k, v, sg, sz, xbc, g, aux = _proj_call(
            h, lp["ng"], lp["w"], lp["wsl"], lp["wg"], lp["bg"], lp["dtb"], lp["alog"], tril2, triu2,
            lp["cw"], lp["cb"], tm, seq_len)
        ygf, ygb = _gla_call(q, k, v, g, n_seq, seq_len, cps)
        ysf, ysb = _ssd_call(xbc, aux, sel_f, sel_b, hmask, n_seq, seq_len, cps)
        h = _merge_call(h, ygf, ygb, ysf, ysb, sg, sz, xbc, p[i].reshape(n_seq * seq_len, D_PLE),
                        lp["og"], lp["dsk"], lp["sng"], lp["wo"], lp["wpg"], lp["wpp"], lp["png"],
                        fng, tm, final=(i == depth - 1))
    return h.reshape(n_seq, seq_len, D_MODEL)


def kernel(x_prompt, x_sample, p_prompt, p_sample, norm_g, w_in, w_gla_gate, b_gla_gate, gla_onorm_g,
           conv_w, conv_b, dt_bias, a_log, d_skip, ssd_norm_g, w_out, w_ple_gate, w_ple_proj,
           ple_norm_g, final_norm_g):
    depth = w_in.shape[0]
    w_packed = _pack_call(w_in)
    layers = [_layer_params(i, w_packed, norm_g, w_in, w_gla_gate, b_gla_gate, gla_onorm_g, conv_w, conv_b,
                            dt_bias, a_log, d_skip, ssd_norm_g, w_out, w_ple_gate, w_ple_proj,
                            ple_norm_g) for i in range(depth)]
    consts = _scan_constants()
    fng = final_norm_g[None, :]
    return (_trunk(x_prompt, p_prompt, layers, fng, consts),
            _trunk(x_sample, p_sample, layers, fng, consts))
```

```python
import functools

import numpy as np
import jax
import jax.numpy as jnp
from jax import lax
from jax.experimental import pallas as pl
from jax.experimental.pallas import tpu as pltpu

F32 = jnp.float32
BF16 = jnp.bfloat16

D_MODEL = 1024
D_PLE = 256
EPS = 1e-6
N_DIR = 2
GLA_HEADS = 4
GLA_KEY = 512
GLA_VAL = 1024
GLA_DK = 128
GLA_DV = 256
GLA_RANK = 16
GLA_TAU = 16.0
SSD_WIDTH = 1024
SSD_HEADDIM = 64
SSD_HEADS = 16
SSD_GROUPS = 4
SSD_HPG = 4
SSD_STATE = 128
SSD_CONV = 5
SSD_GN = SSD_GROUPS * SSD_STATE
SSD_XBC = SSD_WIDTH + 2 * SSD_GN
MIX_WIDTH = GLA_VAL + SSD_WIDTH

LANES = 128
CHUNK = 128
SCAN_CHUNKS_PER_STEP = 4
CONV_SLAB = 256
CONV_ROWS = 128
VMEM_LIMIT = 58 * 1024 * 1024

COL_Q = 0
COL_K = COL_Q + GLA_KEY
COL_V = COL_K + GLA_KEY
COL_ZG = COL_V + GLA_VAL
COL_ZS = COL_ZG + GLA_VAL
COL_XBC = COL_ZS + SSD_WIDTH
PROJ_WIDTH = COL_XBC + SSD_XBC
SRC_LR = 2 * GLA_KEY + 2 * GLA_VAL
SRC_ZS = SRC_LR + N_DIR * GLA_RANK
SRC_DT = SRC_ZS + SSD_WIDTH + SSD_XBC

AUX_CUM, AUX_LDT, AUX_ECUM, AUX_DTAIL = 0, 32, 64, 96
LOG2E = 1.4426950408889634
MIN_LOG2 = -126.0
SEL_ECUM = 0
SEL_DTAIL = SEL_ECUM + SSD_WIDTH
SEL_WIDTH = SEL_DTAIL + SSD_WIDTH


def _sigmoid(x):
    return 1.0 / (1.0 + jnp.exp2(x * (-LOG2E)))


def _silu(x):
    return x * _sigmoid(x)


def _log1pexp_neg_abs(x):
    return jnp.log(1.0 + jnp.exp(-jnp.abs(x)))


def _log_sigmoid(x):
    return jnp.minimum(x, 0.0) - _log1pexp_neg_abs(x)


def _softplus(x):
    return jnp.maximum(x, 0.0) + _log1pexp_neg_abs(x)


def _rms(x, g):
    ms = jnp.mean(x * x, axis=-1, keepdims=True)
    return x * lax.rsqrt(ms + EPS) * g


def _split2(x):
    hi = x.astype(BF16)
    lo = (x - hi.astype(F32)).astype(BF16)
    return hi, lo


def _dot(a, b):
    return jnp.dot(a, b, preferred_element_type=F32)


def _dot_nt(a, b):
    return lax.dot_general(a, b, (((1,), (1,)), ((), ())), preferred_element_type=F32)


def _dot_tn(a, b):
    return lax.dot_general(a, b, (((0,), (0,)), ((), ())), preferred_element_type=F32)


def _const_spec(shape):
    nd = len(shape)
    return pl.BlockSpec(shape, lambda *_: (0,) * nd, pipeline_mode=pl.Buffered(1))


def _pack_kernel(w_ref, o_ref):
    w = w_ref[0]
    o_ref[0, :, :SRC_LR] = w[:, :SRC_LR].astype(BF16)
    o_ref[0, :, SRC_LR:] = w[:, SRC_ZS:SRC_DT].astype(BF16)


def _pack_call(w_in):
    depth, rows, width = w_in.shape
    tr = 256
    return pl.pallas_call(
        _pack_kernel,
        grid=(depth, rows // tr),
        in_specs=[pl.BlockSpec((1, tr, width), lambda l, r: (l, r, 0))],
        out_specs=pl.BlockSpec((1, tr, PROJ_WIDTH), lambda l, r: (l, r, 0)),
        out_shape=jax.ShapeDtypeStruct((depth, rows, PROJ_WIDTH), BF16),
        name="pack",
        compiler_params=pltpu.CompilerParams(dimension_semantics=("parallel", "parallel"),
                                             vmem_limit_bytes=VMEM_LIMIT),
    )(w_in)


def _proj_kernel(h_ref, ng_ref, w_ref, wsl_ref, wg_ref, bg_ref, dtb_ref, alog_ref, tril_ref, triu_ref,
                 cw_ref, cb_ref, q_ref, k_ref, v_ref, sg_ref, sz_ref, xbc_ref, g_ref, aux_ref,
                 bufa_ref, bufb_ref, *, tiles_per_seq):
    i = pl.program_id(0)
    refs = (h_ref, ng_ref, w_ref, wsl_ref, wg_ref, bg_ref, dtb_ref, alog_ref, tril_ref, triu_ref,
            cw_ref, cb_ref, q_ref, k_ref, v_ref, sg_ref, sz_ref, xbc_ref, g_ref, aux_ref)
    seq_start = (i % tiles_per_seq) == 0

    @pl.when(i == 0)
    def _():
        bufb_ref[...] = jnp.zeros(bufb_ref.shape, F32)

    @pl.when(i % 2 == 0)
    def _():
        _proj_tile(*refs, bufa_ref, bufb_ref, seq_start)

    @pl.when(i % 2 == 1)
    def _():
        _proj_tile(*refs, bufb_ref, bufa_ref, seq_start)


def _proj_tile(h_ref, ng_ref, w_ref, wsl_ref, wg_ref, bg_ref, dtb_ref, alog_ref, tril_ref, triu_ref,
               cw_ref, cb_ref, q_ref, k_ref, v_ref, sg_ref, sz_ref, xbc_ref, g_ref, aux_ref,
               cur_ref, prv_ref, seq_start):
    tm = h_ref.shape[0]
    nv = tm // 8
    pad = (SSD_CONV - 1) // 2
    hn = _rms(h_ref[...], ng_ref[...])
    u = hn.astype(BF16)
    u_perm = jnp.swapaxes(hn.reshape(8, nv, D_MODEL), 0, 1).reshape(tm, D_MODEL).astype(BF16)

    def mm(a, b):
        return _dot(u, w_ref[:, a:b])

    slab = _dot(u, wsl_ref[...])
    lr = slab[:, :LANES].astype(BF16)
    la = _log_sigmoid(_dot(lr, wg_ref[...]) + bg_ref[...]) * (LOG2E / GLA_TAU)

    dt = _softplus(slab[:, LANES:] + dtb_ref[...])
    da = dt * (-LOG2E * jnp.exp(alog_ref[...]))
    lane = lax.broadcasted_iota(jnp.int32, (1, LANES), 1)
    is_fwd = (lane % 32) < SSD_HEADS
    grp = lane // 32

    sub = lax.broadcasted_iota(jnp.int32, (8, CONV_SLAB), 0)
    for j in range(SSD_XBC // CONV_SLAB):
        cs = slice(j * CONV_SLAB, (j + 1) * CONV_SLAB)
        xp = _dot(u_perm, w_ref[:, COL_XBC + j * CONV_SLAB:COL_XBC + (j + 1) * CONV_SLAB])
        cur_ref[8 * pad:8 * pad + tm, cs] = xp
        for e in range(pad):
            src = nv - pad + e
            below = prv_ref[8 * pad + 8 * src + 7:8 * pad + 8 * src + 8, cs]
            moved = pltpu.roll(xp[8 * src:8 * src + 8], 1, axis=0)
            cur_ref[8 * e:8 * e + 8, cs] = jnp.where(sub == 0, jnp.where(seq_start, 0.0, below), moved)
            first = prv_ref[8 * pad + 8 * e:8 * pad + 8 * e + 8, cs]
            moved = pltpu.roll(first, 7, axis=0)
            above = xp[8 * e:8 * e + 1]
            prv_ref[8 * pad + tm + 8 * e:8 * pad + tm + 8 * e + 8, cs] = jnp.where(
                sub == 7, jnp.where(seq_start, 0.0, above), moved)

    tril = tril_ref[...]
    triu = triu_ref[...]
    for c in range(tm // CHUNK):
        sl = slice(c * CHUNK, (c + 1) * CHUNK)
        hi, lo = _split2(la[sl])
        st = jnp.concatenate([hi, lo], axis=0)
        g_ref[sl, :GLA_KEY] = _dot(tril, st[:, :GLA_KEY])
        g_ref[sl, GLA_KEY:] = _dot(triu, st[:, GLA_KEY:])

        dhi, dlo = _split2(da[sl])
        dst = jnp.concatenate([dhi, dlo], axis=0)
        cum = jnp.where(is_fwd, _dot(tril, dst), _dot(triu, dst))
        tot = jnp.where(is_fwd, cum[CHUNK - 1:CHUNK], cum[0:1])
        dtc = dt[sl]
        aux = jnp.where(grp == 0, cum,
                        jnp.where(grp == 1, jnp.maximum(jnp.log2(dtc), MIN_LOG2),
                                  jnp.where(grp == 2, jnp.exp2(cum), dtc * jnp.exp2(tot - cum))))
        aux_ref[sl, :] = aux

    def conv_slab(j):
        cs = slice(j * CONV_SLAB, (j + 1) * CONV_SLAB)
        for r0 in range(0, tm, CONV_ROWS):
            acc = cb_ref[:, cs] + cw_ref[0:1, cs] * prv_ref[r0:r0 + CONV_ROWS, cs]
            for tap in range(1, SSD_CONV):
                acc = acc + cw_ref[tap:tap + 1, cs] * prv_ref[r0 + 8 * tap:r0 + 8 * tap + CONV_ROWS, cs]
            act = jnp.swapaxes(_silu(acc).reshape(CONV_ROWS // 8, 8, CONV_SLAB), 0, 1).astype(BF16)
            for s8 in range(8):
                xbc_ref[nv * s8 + r0 // 8:nv * s8 + (r0 + CONV_ROWS) // 8, cs] = act[s8]

    half = GLA_VAL // 2
    pieces = [(q_ref, 0, COL_Q, lambda r: r * (GLA_DK ** -0.5)), (k_ref, 0, COL_K, None),
              (v_ref, 0, COL_V, None), (v_ref, half, COL_V + half, None),
              (sg_ref, 0, COL_ZG, _silu), (sg_ref, half, COL_ZG + half, _silu),
              (sz_ref, 0, COL_ZS, _silu), (sz_ref, half, COL_ZS + half, _silu)]
    assert len(pieces) == SSD_XBC // CONV_SLAB and GLA_KEY == half
    for j, (ref, lo, col, post) in enumerate(pieces):
        r = mm(col, col + half)
        ref[:, lo:lo + half] = (post(r) if post else r).astype(BF16)
        conv_slab(j)


def _proj_call(h, ng, w, wsl, wg, bg, dtb, alog, tril, triu, cw, cb, tm, seq_len):
    t = h.shape[0]
    nb = t // tm
    row = lambda width: pl.BlockSpec((tm, width), lambda i: (jnp.minimum(i, nb - 1), 0))
    conv_row = pl.BlockSpec((tm, SSD_XBC), lambda i: (jnp.maximum(i - 1, 0), 0))
    consts = (ng, w, wsl, wg, bg, dtb, alog, tril, triu, cw, cb)
    out_shape = (
        jax.ShapeDtypeStruct((t, GLA_KEY), BF16), jax.ShapeDtypeStruct((t, GLA_KEY), BF16),
        jax.ShapeDtypeStruct((t, GLA_VAL), BF16), jax.ShapeDtypeStruct((t, GLA_VAL), BF16),
        jax.ShapeDtypeStruct((t, SSD_WIDTH), BF16), jax.ShapeDtypeStruct((t, SSD_XBC), BF16),
        jax.ShapeDtypeStruct((t, 2 * GLA_KEY), F32), jax.ShapeDtypeStruct((t, LANES), F32),
    )
    return pl.pallas_call(
        functools.partial(_proj_kernel, tiles_per_seq=seq_len // tm),
        grid=(nb + 1,),
        in_specs=[row(D_MODEL)] + [_const_spec(c.shape) for c in consts],
        out_specs=(row(GLA_KEY), row(GLA_KEY), row(GLA_VAL), row(GLA_VAL), row(SSD_WIDTH),
                   conv_row, row(2 * GLA_KEY), row(LANES)),
        out_shape=out_shape,
        scratch_shapes=[pltpu.VMEM((tm + 8 * (SSD_CONV - 1), SSD_XBC), F32)] * 2,
        name="proj",
        compiler_params=pltpu.CompilerParams(dimension_semantics=("arbitrary",),
                                             vmem_limit_bytes=VMEM_LIMIT),
    )(h, *consts)


def _pair_mask(reverse):
    row = lax.broadcasted_iota(jnp.int32, (CHUNK, CHUNK), 0)
    col = lax.broadcasted_iota(jnp.int32, (CHUNK, CHUNK), 1)
    return (col > row) if reverse else (col <= row)


def _gla_chunks(jobs):
    half = CHUNK // 2
    work = []
    for q_ref, k_ref, v_ref, g_ref, y_ref, s_ref, c, reverse in jobs:
        rows = slice(c * CHUNK, (c + 1) * CHUNK)
        mid = half if reverse else half - 1
        end = 0 if reverse else CHUNK - 1
        for h in range(GLA_HEADS):
            ks = slice(h * GLA_DK, (h + 1) * GLA_DK)
            g = g_ref[rows, ks]
            ref = g[mid:mid + 1]
            gtot = g[end:end + 1]
            d = g - ref
            qt = q_ref[rows, ks].astype(F32) * jnp.exp2(d)
            kt = k_ref[rows, ks].astype(F32) * jnp.exp2(-d)
            att = _dot_nt(qt.astype(BF16), kt.astype(BF16))
            work.append(dict(att=att, qs=(qt * jnp.exp2(ref)).astype(BF16),
                             kk=(kt * jnp.exp2(gtot - ref)).astype(BF16), etot=jnp.exp2(gtot),
                             v_ref=v_ref, y_ref=y_ref, s_ref=s_ref, rows=rows, h=h, reverse=reverse))
    for w in work:
        vs = slice(w["h"] * GLA_DV, (w["h"] + 1) * GLA_DV)
        att = jnp.where(_pair_mask(w["reverse"]), w["att"], 0.0).astype(BF16)
        y = _dot(jnp.concatenate([att, w["qs"]], axis=1),
                 jnp.concatenate([w["v_ref"][w["rows"], vs], w["s_ref"][w["h"]].astype(BF16)], axis=0))
        w["y_ref"][w["rows"], vs] = y.astype(BF16)
    for w in work:
        vs = slice(w["h"] * GLA_DV, (w["h"] + 1) * GLA_DV)
        ecol = jnp.broadcast_to(w["etot"], (GLA_DK, GLA_DK)).T
        upd = _dot_tn(w["kk"], w["v_ref"][w["rows"], vs])
        w["s_ref"][w["h"]] = w["s_ref"][w["h"]] * jnp.concatenate([ecol, ecol], axis=1) + upd


def _gla_kernel(qf_ref, kf_ref, vf_ref, gf_ref, qb_ref, kb_ref, vb_ref, gb_ref,
                yf_ref, yb_ref, sf_ref, sb_ref, *, cps):
    @pl.when(pl.program_id(1) == 0)
    def _():
        sf_ref[...] = jnp.zeros_like(sf_ref)
        sb_ref[...] = jnp.zeros_like(sb_ref)

    for c in range(cps):
        _gla_chunks([(qf_ref, kf_ref, vf_ref, gf_ref, yf_ref, sf_ref, c, False),
                     (qb_ref, kb_ref, vb_ref, gb_ref, yb_ref, sb_ref, cps - 1 - c, True)])


def _scan_specs(width, nb, rows, col=0):
    fwd = pl.BlockSpec((rows, width), lambda s, j: (s * nb + j, col))
    bwd = pl.BlockSpec((rows, width), lambda s, j: (s * nb + (nb - 1 - j), col))
    return fwd, bwd


def _gla_call(q, k, v, g, n_seq, seq_len, cps):
    t = q.shape[0]
    rows = cps * CHUNK
    nb = seq_len // rows
    kf, kb = _scan_specs(GLA_KEY, nb, rows)
    vf, vb = _scan_specs(GLA_VAL, nb, rows)
    gf, _ = _scan_specs(GLA_KEY, nb, rows, 0)
    _, gb = _scan_specs(GLA_KEY, nb, rows, 1)
    state = pltpu.VMEM((GLA_HEADS, GLA_DK, GLA_DV), F32)
    return pl.pallas_call(
        functools.partial(_gla_kernel, cps=cps),
        grid=(n_seq, nb),
        in_specs=[kf, kf, vf, gf, kb, kb, vb, gb],
        out_specs=(vf, vb),
        out_shape=(jax.ShapeDtypeStruct((t, GLA_VAL), BF16),) * 2,
        scratch_shapes=[state, state],
        name="gla",
        compiler_params=pltpu.CompilerParams(dimension_semantics=("arbitrary", "arbitrary"),
                                             vmem_limit_bytes=VMEM_LIMIT),
    )(q, k, v, g, q, k, v, g)


def _ssd_chunks(jobs, hmask_ref):
    gw = SSD_HPG * SSD_HEADDIM
    work = []
    for xbc_ref, aux_ref, sel_ref, y_ref, h_ref, c, reverse in jobs:
        rows = slice(c * CHUNK, (c + 1) * CHUNK)
        off = SSD_HEADS if reverse else 0
        hi, lo = _split2(aux_ref[rows, :])
        hl = jnp.concatenate([hi, lo], axis=1)
        aux_r = hi.astype(F32) + lo.astype(F32)
        aux_t = aux_r.T
        ldt_j = aux_t[AUX_LDT + off:AUX_LDT + off + SSD_HEADS]
        rel_j = ldt_j - aux_t[AUX_CUM + off:AUX_CUM + off + SSD_HEADS]
        for g in range(SSD_GROUPS):
            bg = xbc_ref[rows, SSD_WIDTH + g * SSD_STATE:SSD_WIDTH + (g + 1) * SSD_STATE]
            cg = xbc_ref[rows, SSD_WIDTH + SSD_GN + g * SSD_STATE:SSD_WIDTH + SSD_GN + (g + 1) * SSD_STATE]
            work.append(dict(cb=_dot_nt(cg, bg), bg=bg, cg=cg, g=g, rows=rows, reverse=reverse, off=off,
                             hl=hl, aux_r=aux_r, ldt_j=ldt_j, rel_j=rel_j, sel_ref=sel_ref,
                             xbc_ref=xbc_ref, y_ref=y_ref, h_ref=h_ref))
    def expand(w):
        lo_col = w["g"] * gw
        return (_dot(w["hl"], w["sel_ref"][:, SEL_ECUM + lo_col:SEL_ECUM + lo_col + gw]),
                _dot(w["hl"], w["sel_ref"][:, SEL_DTAIL + lo_col:SEL_DTAIL + lo_col + gw]))

    expanded = {0: expand(work[0])}
    for idx, w in enumerate(work):
        if idx + 1 < len(work):
            expanded[idx + 1] = expand(work[idx + 1])
        ecum, dtail = expanded.pop(idx)
        g, off = w["g"], w["off"]
        gs = slice(g * gw, (g + 1) * gw)
        end = 0 if w["reverse"] else CHUNK - 1
        cb = jnp.where(_pair_mask(w["reverse"]), w["cb"], 0.0)
        ws = []
        for r in range(SSD_HPG):
            hd = g * SSD_HPG + r
            cum_i = jnp.broadcast_to(w["aux_r"][:, AUX_CUM + off + hd:AUX_CUM + off + hd + 1], (CHUNK, CHUNK))
            e = jnp.minimum(cum_i + w["rel_j"][hd:hd + 1], w["ldt_j"][hd:hd + 1])
            ws.append((cb * jnp.exp2(e)).astype(BF16))
        xg = w["xbc_ref"][w["rows"], gs]
        xbd = jnp.concatenate([xg] * SSD_HPG, axis=0) * hmask_ref[...]
        hg = w["h_ref"][g]
        y = _dot(jnp.concatenate(ws, axis=1), xbd) + _dot(w["cg"], hg.astype(BF16)) * ecum
        w["y_ref"][w["rows"], gs] = y.astype(BF16)
        xt = (xg.astype(F32) * dtail).astype(BF16)
        w["h_ref"][g] = hg * ecum[end:end + 1] + _dot_tn(w["bg"], xt)


def _ssd_kernel(xf_ref, af_ref, xb_ref, ab_ref, self_ref, selb_ref, hmask_ref,
                yf_ref, yb_ref, hf_ref, hb_ref, *, cps):
    @pl.when(pl.program_id(1) == 0)
    def _():
        hf_ref[...] = jnp.zeros_like(hf_ref)
        hb_ref[...] = jnp.zeros_like(hb_ref)

    for c in range(cps):
        _ssd_chunks([(xf_ref, af_ref, self_ref, yf_ref, hf_ref, c, False),
                     (xb_ref, ab_ref, selb_ref, yb_ref, hb_ref, cps - 1 - c, True)], hmask_ref)


def _ssd_call(xbc, aux, sel_f, sel_b, hmask, n_seq, seq_len, cps):
    t = xbc.shape[0]
    rows = cps * CHUNK
    nb = seq_len // rows
    xf, xb = _scan_specs(SSD_XBC, nb, rows)
    af, ab = _scan_specs(LANES, nb, rows)
    yf, yb = _scan_specs(SSD_WIDTH, nb, rows)
    state = pltpu.VMEM((SSD_GROUPS, SSD_STATE, SSD_HPG * SSD_HEADDIM), F32)
    return pl.pallas_call(
        functools.partial(_ssd_kernel, cps=cps),
        grid=(n_seq, nb),
        in_specs=[xf, af, xb, ab, _const_spec(sel_f.shape), _const_spec(sel_b.shape),
                  _const_spec(hmask.shape)],
        out_specs=(yf, yb),
        out_shape=(jax.ShapeDtypeStruct((t, SSD_WIDTH), BF16),) * 2,
        scratch_shapes=[state, state],
        name="ssd",
        compiler_params=pltpu.CompilerParams(dimension_semantics=("arbitrary", "arbitrary"),
                                             vmem_limit_bytes=VMEM_LIMIT),
    )(xbc, aux, xbc, aux, sel_f, sel_b, hmask)


def _merge_kernel(h_ref, ygf_ref, ygb_ref, ysf_ref, ysb_ref, sg_ref, sz_ref, xs_ref, p_ref,
                  og_ref, dsk_ref, sng_ref, wo_ref, wpg_ref, wpp_ref, png_ref, fng_ref,
                  o_ref, *, final):
    o = ygf_ref[...].astype(F32) + ygb_ref[...].astype(F32)
    og = og_ref[...]
    heads = [_rms(o[:, h * GLA_DV:(h + 1) * GLA_DV], og) for h in range(GLA_HEADS)]
    o = jnp.concatenate(heads, axis=1) * sg_ref[...].astype(F32)
    y = ysf_ref[...].astype(F32) + ysb_ref[...].astype(F32) + dsk_ref[...] * xs_ref[...].astype(F32)
    y = _rms(y * sz_ref[...].astype(F32), sng_ref[...])
    h = h_ref[...] + _dot(o.astype(BF16), wo_ref[:GLA_VAL, :]) + _dot(y.astype(BF16), wo_ref[GLA_VAL:, :])
    e = _rms(_dot(p_ref[...].astype(BF16), wpp_ref[...]), png_ref[...])
    h = h + _sigmoid(_dot(h.astype(BF16), wpg_ref[...])) * e
    if final:
        h = _rms(h, fng_ref[...])
    o_ref[...] = h


def _merge_call(h, ygf, ygb, ysf, ysb, sg, sz, xbc, p, og, dsk, sng, wo, wpg, wpp, png, fng, tm, final):
    t = h.shape[0]
    row = lambda width: pl.BlockSpec((tm, width), lambda i: (i, 0))
    consts = (og, dsk, sng, wo, wpg, wpp, png, fng)
    return pl.pallas_call(
        functools.partial(_merge_kernel, final=final),
        grid=(t // tm,),
        in_specs=[row(D_MODEL)] + [row(GLA_VAL)] * 2 + [row(SSD_WIDTH)] * 2
                 + [row(GLA_VAL), row(SSD_WIDTH), row(SSD_WIDTH), row(D_PLE)]
                 + [_const_spec(c.shape) for c in consts],
        out_specs=row(D_MODEL),
        out_shape=jax.ShapeDtypeStruct((t, D_MODEL), F32),
        name="merge",
        compiler_params=pltpu.CompilerParams(dimension_semantics=("parallel",),
                                             vmem_limit_bytes=VMEM_LIMIT),
    )(h, ygf, ygb, ysf, ysb, sg, sz, xbc, p, *consts)


def _scan_constants():
    r = np.arange(CHUNK)
    tril = (r[None, :] <= r[:, None]).astype(np.float32)
    triu = (r[None, :] >= r[:, None]).astype(np.float32)
    tril2 = jnp.asarray(np.concatenate([tril, tril], axis=1), BF16)
    triu2 = jnp.asarray(np.concatenate([triu, triu], axis=1), BF16)
    sels = []
    for off in (0, SSD_HEADS):
        sel = np.zeros((LANES, SEL_WIDTH), np.float32)
        for hd in range(SSD_HEADS):
            sel[AUX_ECUM + off + hd, SEL_ECUM + hd * SSD_HEADDIM:SEL_ECUM + (hd + 1) * SSD_HEADDIM] = 1.0
            sel[AUX_DTAIL + off + hd, SEL_DTAIL + hd * SSD_HEADDIM:SEL_DTAIL + (hd + 1) * SSD_HEADDIM] = 1.0
        sels.append(jnp.asarray(np.concatenate([sel, sel], axis=0), BF16))
    gw = SSD_HPG * SSD_HEADDIM
    hmask = np.zeros((SSD_HPG * CHUNK, gw), np.float32)
    for rr in range(SSD_HPG):
        hmask[rr * CHUNK:(rr + 1) * CHUNK, rr * SSD_HEADDIM:(rr + 1) * SSD_HEADDIM] = 1.0
    return tril2, triu2, sels[0], sels[1], jnp.asarray(hmask, BF16)


def _layer_params(i, w_packed, norm_g, w_in, w_gla_gate, b_gla_gate, gla_onorm_g, conv_w, conv_b, dt_bias,
                  a_log, d_skip, ssd_norm_g, w_out, w_ple_gate, w_ple_proj, ple_norm_g):
    reps = LANES // (N_DIR * SSD_HEADS)
    lr_slab = jnp.pad(w_in[i, :, SRC_LR:SRC_ZS], ((0, 0), (0, LANES - N_DIR * GLA_RANK)))
    dt_slab = jnp.tile(w_in[i, :, SRC_DT:], (1, reps))
    wg = jnp.zeros((LANES, N_DIR * GLA_KEY), F32)
    wg = wg.at[:GLA_RANK, :GLA_KEY].set(w_gla_gate[i, 0])
    wg = wg.at[GLA_RANK:2 * GLA_RANK, GLA_KEY:].set(w_gla_gate[i, 1])
    return dict(
        ng=norm_g[i][None, :], w=w_packed[i], wsl=jnp.concatenate([lr_slab, dt_slab], axis=1).astype(BF16),
        wg=wg.astype(BF16), bg=b_gla_gate[i].reshape(1, -1),
        dtb=jnp.tile(dt_bias[i].reshape(1, -1), (1, reps)), alog=jnp.tile(a_log[i].reshape(1, -1), (1, reps)),
        cw=conv_w[i], cb=conv_b[i][None, :], og=gla_onorm_g[i][None, :],
        dsk=jnp.repeat(d_skip[i], SSD_HEADDIM)[None, :], sng=ssd_norm_g[i][None, :],
        wo=w_out[i].astype(BF16), wpg=w_ple_gate[i].astype(BF16), wpp=w_ple_proj[i].astype(BF16),
        png=ple_norm_g[i][None, :])


def _tile(seq_len, want):
    tm = min(want, seq_len)
    assert seq_len % tm == 0 and tm % CHUNK == 0
    return tm


def _trunk(x, p, layers, fng, consts):
    n_seq, seq_len, _ = x.shape
    tril2, triu2, sel_f, sel_b, hmask = consts
    h = x.reshape(n_seq * seq_len, D_MODEL)
    tm = _tile(seq_len, 512)
    cps = min(SCAN_CHUNKS_PER_STEP, seq_len // CHUNK)
    depth = len(layers)
    for i, lp in enumerate(layers):
        q, k, v, sg, sz, xbc, g, aux = _proj_call(
            h, lp["ng"], lp["w"], lp["wsl"], lp["wg"], lp["bg"], lp["dtb"], lp["alog"], tril2, triu2,
            lp["cw"], lp["cb"], tm, seq_len)
        ygf, ygb = _gla_call(q, k, v, g, n_seq, seq_len, cps)
        ysf, ysb = _ssd_call(xbc, aux, sel_f, sel_b, hmask, n_seq, seq_len, cps)
        h = _merge_call(h, ygf, ygb, ysf, ysb, sg, sz, xbc, p[i].reshape(n_seq * seq_len, D_PLE),
                        lp["og"], lp["dsk"], lp["sng"], lp["wo"], lp["wpg"], lp["wpp"], lp["png"],
                        fng, tm, final=(i == depth - 1))
    return h.reshape(n_seq, seq_len, D_MODEL)


def kernel(x_prompt, x_sample, p_prompt, p_sample, norm_g, w_in, w_gla_gate, b_gla_gate, gla_onorm_g,
           conv_w, conv_b, dt_bias, a_log, d_skip, ssd_norm_g, w_out, w_ple_gate, w_ple_proj,
           ple_norm_g, final_norm_g):
    depth = w_in.shape[0]
    w_packed = _pack_call(w_in)
    layers = [_layer_params(i, w_packed, norm_g, w_in, w_gla_gate, b_gla_gate, gla_onorm_g, conv_w, conv_b,
                            dt_bias, a_log, d_skip, ssd_norm_g, w_out, w_ple_gate, w_ple_proj,
                            ple_norm_g) for i in range(depth)]
    consts = _scan_constants()
    fng = final_norm_g[None, :]
    return (_trunk(x_prompt, p_prompt, layers, fng, consts),
            _trunk(x_sample, p_sample, layers, fng, consts))
```

```python
import functools

import numpy as np
import jax
import jax.numpy as jnp
from jax import lax
from jax.experimental import pallas as pl
from jax.experimental.pallas import tpu as pltpu

F32 = jnp.float32
BF16 = jnp.bfloat16

D_MODEL = 1024
D_PLE = 256
EPS = 1e-6
N_DIR = 2
GLA_HEADS = 4
GLA_KEY = 512
GLA_VAL = 1024
GLA_DK = 128
GLA_DV = 256
GLA_RANK = 16
GLA_TAU = 16.0
SSD_WIDTH = 1024
SSD_HEADDIM = 64
SSD_HEADS = 16
SSD_GROUPS = 4
SSD_HPG = 4
SSD_STATE = 128
SSD_CONV = 5
SSD_GN = SSD_GROUPS * SSD_STATE
SSD_XBC = SSD_WIDTH + 2 * SSD_GN
MIX_WIDTH = GLA_VAL + SSD_WIDTH

LANES = 128
CHUNK = 128
SCAN_CHUNKS_PER_STEP = 4
CONV_SLAB = 256
CONV_ROWS = 128
VMEM_LIMIT = 58 * 1024 * 1024

COL_Q = 0
COL_K = COL_Q + GLA_KEY
COL_V = COL_K + GLA_KEY
COL_ZG = COL_V + GLA_VAL
COL_ZS = COL_ZG + GLA_VAL
COL_XBC = COL_ZS + SSD_WIDTH
PROJ_WIDTH = COL_XBC + SSD_XBC
SRC_LR = 2 * GLA_KEY + 2 * GLA_VAL
SRC_ZS = SRC_LR + N_DIR * GLA_RANK
SRC_DT = SRC_ZS + SSD_WIDTH + SSD_XBC

COL_GF = 0
COL_GB = COL_GF + GLA_KEY
COL_AUX = COL_GB + GLA_KEY
GA_WIDTH = COL_AUX + LANES

AUX_CUM, AUX_LDT, AUX_ECUM, AUX_DTAIL = 0, 32, 64, 96
LOG2E = 1.4426950408889634
MIN_LOG2 = -126.0
SEL_ECUM = 0
SEL_DTAIL = SEL_ECUM + SSD_WIDTH
SEL_WIDTH = SEL_DTAIL + SSD_WIDTH


def _sigmoid(x):
    return 1.0 / (1.0 + jnp.exp2(x * (-LOG2E)))


def _silu(x):
    return x * _sigmoid(x)


def _log1pexp_neg_abs(x):
    return jnp.log(1.0 + jnp.exp(-jnp.abs(x)))


def _log_sigmoid(x):
    return jnp.minimum(x, 0.0) - _log1pexp_neg_abs(x)


def _softplus(x):
    return jnp.maximum(x, 0.0) + _log1pexp_neg_abs(x)


def _rms(x, g):
    ms = jnp.mean(x * x, axis=-1, keepdims=True)
    return x * lax.rsqrt(ms + EPS) * g


def _split2(x):
    hi = x.astype(BF16)
    lo = (x - hi.astype(F32)).astype(BF16)
    return hi, lo


def _dot(a, b):
    return jnp.dot(a, b, preferred_element_type=F32)


def _dot_nt(a, b):
    return lax.dot_general(a, b, (((1,), (1,)), ((), ())), preferred_element_type=F32)


def _dot_tn(a, b):
    return lax.dot_general(a, b, (((0,), (0,)), ((), ())), preferred_element_type=F32)


def _const_spec(shape):
    nd = len(shape)
    return pl.BlockSpec(shape, lambda *_: (0,) * nd, pipeline_mode=pl.Buffered(1))


def _pack_kernel(w_ref, o_ref):
    w = w_ref[0]
    o_ref[0, :, :SRC_LR] = w[:, :SRC_LR].astype(BF16)
    o_ref[0, :, SRC_LR:] = w[:, SRC_ZS:SRC_DT].astype(BF16)


def _pack_call(w_in):
    depth, rows, width = w_in.shape
    tr = 256
    return pl.pallas_call(
        _pack_kernel,
        grid=(depth, rows // tr),
        in_specs=[pl.BlockSpec((1, tr, width), lambda l, r: (l, r, 0))],
        out_specs=pl.BlockSpec((1, tr, PROJ_WIDTH), lambda l, r: (l, r, 0)),
        out_shape=jax.ShapeDtypeStruct((depth, rows, PROJ_WIDTH), BF16),
        name="pack",
        compiler_params=pltpu.CompilerParams(dimension_semantics=("parallel", "parallel"),
                                             vmem_limit_bytes=VMEM_LIMIT),
    )(w_in)


def _proj_kernel(h_ref, ng_ref, w_ref, wsl_ref, wg_ref, bg_ref, dtb_ref, alog_ref, tril_ref, triu_ref,
                 cw_ref, cb_ref, act_ref, xbc_ref, ga_ref, bufa_ref, bufb_ref, *, tiles_per_seq):
    i = pl.program_id(0)
    refs = (h_ref, ng_ref, w_ref, wsl_ref, wg_ref, bg_ref, dtb_ref, alog_ref, tril_ref, triu_ref,
            cw_ref, cb_ref, act_ref, xbc_ref, ga_ref)
    seq_start = (i % tiles_per_seq) == 0

    @pl.when(i == 0)
    def _():
        bufb_ref[...] = jnp.zeros(bufb_ref.shape, F32)

    @pl.when(i % 2 == 0)
    def _():
        _proj_tile(*refs, bufa_ref, bufb_ref, seq_start)

    @pl.when(i % 2 == 1)
    def _():
        _proj_tile(*refs, bufb_ref, bufa_ref, seq_start)


def _proj_tile(h_ref, ng_ref, w_ref, wsl_ref, wg_ref, bg_ref, dtb_ref, alog_ref, tril_ref, triu_ref,
               cw_ref, cb_ref, act_ref, xbc_ref, ga_ref, cur_ref, prv_ref, seq_start):
    tm = h_ref.shape[0]
    nv = tm // 8
    pad = (SSD_CONV - 1) // 2
    hn = _rms(h_ref[...], ng_ref[...])
    u = hn.astype(BF16)
    u_perm = jnp.swapaxes(hn.reshape(8, nv, D_MODEL), 0, 1).reshape(tm, D_MODEL).astype(BF16)

    def mm(a, b):
        return _dot(u, w_ref[:, a:b])

    slab = _dot(u, wsl_ref[...].astype(BF16))
    lr = slab[:, :LANES].astype(BF16)
    la = _log_sigmoid(_dot(lr, wg_ref[...]) + bg_ref[...]) * (LOG2E / GLA_TAU)

    dt = _softplus(slab[:, LANES:] + dtb_ref[...])
    da = dt * (-LOG2E * jnp.exp(alog_ref[...]))
    lane = lax.broadcasted_iota(jnp.int32, (1, LANES), 1)
    is_fwd = (lane % 32) < SSD_HEADS
    grp = lane // 32

    sub = lax.broadcasted_iota(jnp.int32, (8, CONV_SLAB), 0)
    for j in range(SSD_XBC // CONV_SLAB):
        cs = slice(j * CONV_SLAB, (j + 1) * CONV_SLAB)
        xp = _dot(u_perm, w_ref[:, COL_XBC + j * CONV_SLAB:COL_XBC + (j + 1) * CONV_SLAB])
        cur_ref[8 * pad:8 * pad + tm, cs] = xp
        for e in range(pad):
            src = nv - pad + e
            below = prv_ref[8 * pad + 8 * src + 7:8 * pad + 8 * src + 8, cs]
            moved = pltpu.roll(xp[8 * src:8 * src + 8], 1, axis=0)
            cur_ref[8 * e:8 * e + 8, cs] = jnp.where(sub == 0, jnp.where(seq_start, 0.0, below), moved)
            first = prv_ref[8 * pad + 8 * e:8 * pad + 8 * e + 8, cs]
            moved = pltpu.roll(first, 7, axis=0)
            above = xp[8 * e:8 * e + 1]
            prv_ref[8 * pad + tm + 8 * e:8 * pad + tm + 8 * e + 8, cs] = jnp.where(
                sub == 7, jnp.where(seq_start, 0.0, above), moved)

    tril = tril_ref[...]
    triu = triu_ref[...]
    for c in range(tm // CHUNK):
        sl = slice(c * CHUNK, (c + 1) * CHUNK)
        hi, lo = _split2(la[sl])
        st = jnp.concatenate([hi, lo], axis=0)
        ga_ref[sl, COL_GF:COL_GB] = _dot(tril, st[:, :GLA_KEY])
        ga_ref[sl, COL_GB:COL_AUX] = _dot(triu, st[:, GLA_KEY:])

        dhi, dlo = _split2(da[sl])
        dst = jnp.concatenate([dhi, dlo], axis=0)
        cum = jnp.where(is_fwd, _dot(tril, dst), _dot(triu, dst))
        tot = jnp.where(is_fwd, cum[CHUNK - 1:CHUNK], cum[0:1])
        dtc = dt[sl]
        aux = jnp.where(grp == 0, cum,
                        jnp.where(grp == 1, jnp.maximum(jnp.log2(dtc), MIN_LOG2),
                                  jnp.where(grp == 2, jnp.exp2(cum), dtc * jnp.exp2(tot - cum))))
        ga_ref[sl, COL_AUX:] = aux

    def conv_slab(j):
        cs = slice(j * CONV_SLAB, (j + 1) * CONV_SLAB)
        for r0 in range(0, tm, CONV_ROWS):
            acc = cb_ref[:, cs] + cw_ref[0:1, cs] * prv_ref[r0:r0 + CONV_ROWS, cs]
            for tap in range(1, SSD_CONV):
                acc = acc + cw_ref[tap:tap + 1, cs] * prv_ref[r0 + 8 * tap:r0 + 8 * tap + CONV_ROWS, cs]
            act = jnp.swapaxes(_silu(acc).reshape(CONV_ROWS // 8, 8, CONV_SLAB), 0, 1).astype(BF16)
            for s8 in range(8):
                xbc_ref[nv * s8 + r0 // 8:nv * s8 + (r0 + CONV_ROWS) // 8, cs] = act[s8]

    scale_q = lambda r: r * (GLA_DK ** -0.5)
    posts = [scale_q, None, None, None, _silu, _silu, _silu, _silu]
    width = COL_XBC // len(posts)
    assert len(posts) == SSD_XBC // CONV_SLAB and width == GLA_KEY
    for j, post in enumerate(posts):
        r = mm(j * width, (j + 1) * width)
        act_ref[:, j * width:(j + 1) * width] = (post(r) if post else r).astype(BF16)
        conv_slab(j)


def _proj_call(h, layer, ng, w, wsl, wg, bg, dtb, alog, tril, triu, cw, cb, tm, seq_len):
    t = h.shape[0]
    nb = t // tm
    row = lambda width: pl.BlockSpec((tm, width), lambda i: (jnp.minimum(i, nb - 1), 0))
    conv_row = pl.BlockSpec((tm, SSD_XBC), lambda i: (jnp.maximum(i - 1, 0), 0))
    consts = (wsl, wg, bg, dtb, alog, tril, triu, cw, cb)
    w_spec = pl.BlockSpec((None,) + w.shape[1:], lambda i: (layer, 0, 0), pipeline_mode=pl.Buffered(1))
    out_shape = (
        jax.ShapeDtypeStruct((t, COL_XBC), BF16), jax.ShapeDtypeStruct((t, SSD_XBC), BF16),
        jax.ShapeDtypeStruct((t, GA_WIDTH), F32),
    )
    return pl.pallas_call(
        functools.partial(_proj_kernel, tiles_per_seq=seq_len // tm),
        grid=(nb + 1,),
        in_specs=[row(D_MODEL), _const_spec(ng.shape), w_spec] + [_const_spec(c.shape) for c in consts],
        out_specs=(row(COL_XBC), conv_row, row(GA_WIDTH)),
        out_shape=out_shape,
        scratch_shapes=[pltpu.VMEM((tm + 8 * (SSD_CONV - 1), SSD_XBC), F32)] * 2,
        name="proj",
        compiler_params=pltpu.CompilerParams(dimension_semantics=("arbitrary",),
                                             vmem_limit_bytes=VMEM_LIMIT),
    )(h, ng, w, *consts)


def _pair_mask(reverse):
    row = lax.broadcasted_iota(jnp.int32, (CHUNK, CHUNK), 0)
    col = lax.broadcasted_iota(jnp.int32, (CHUNK, CHUNK), 1)
    return (col > row) if reverse else (col <= row)


def _gla_chunks(jobs):
    half = CHUNK // 2
    work = []
    for q_ref, k_ref, v_ref, g_ref, y_ref, s_ref, c, reverse in jobs:
        rows = slice(c * CHUNK, (c + 1) * CHUNK)
        mid = half if reverse else half - 1
        end = 0 if reverse else CHUNK - 1
        for h in range(GLA_HEADS):
            ks = slice(h * GLA_DK, (h + 1) * GLA_DK)
            g = g_ref[rows, ks]
            ref = g[mid:mid + 1]
            gtot = g[end:end + 1]
            d = g - ref
            qt = q_ref[rows, ks].astype(F32) * jnp.exp2(d)
            kt = k_ref[rows, ks].astype(F32) * jnp.exp2(-d)
            att = _dot_nt(qt.astype(BF16), kt.astype(BF16))
            work.append(dict(att=att, qs=(qt * jnp.exp2(ref)).astype(BF16),
                             kk=(kt * jnp.exp2(gtot - ref)).astype(BF16), etot=jnp.exp2(gtot),
                             v_ref=v_ref, y_ref=y_ref, s_ref=s_ref, rows=rows, h=h, reverse=reverse))
    for w in work:
        vs = slice(w["h"] * GLA_DV, (w["h"] + 1) * GLA_DV)
        att = jnp.where(_pair_mask(w["reverse"]), w["att"], 0.0).astype(BF16)
        y = _dot(jnp.concatenate([att, w["qs"]], axis=1),
                 jnp.concatenate([w["v_ref"][w["rows"], vs], w["s_ref"][w["h"]].astype(BF16)], axis=0))
        w["y_ref"][w["rows"], vs] = y.astype(BF16)
    for w in work:
        vs = slice(w["h"] * GLA_DV, (w["h"] + 1) * GLA_DV)
        ecol = jnp.broadcast_to(w["etot"], (GLA_DK, GLA_DK)).T
        upd = _dot_tn(w["kk"], w["v_ref"][w["rows"], vs])
        w["s_ref"][w["h"]] = w["s_ref"][w["h"]] * jnp.concatenate([ecol, ecol], axis=1) + upd


def _gla_kernel(qf_ref, kf_ref, vf_ref, gf_ref, qb_ref, kb_ref, vb_ref, gb_ref,
                yf_ref, yb_ref, sf_ref, sb_ref, *, cps):
    @pl.when(pl.program_id(1) == 0)
    def _():
        sf_ref[...] = jnp.zeros_like(sf_ref)
        sb_ref[...] = jnp.zeros_like(sb_ref)

    for c in range(cps):
        _gla_chunks([(qf_ref, kf_ref, vf_ref, gf_ref, yf_ref, sf_ref, c, False),
                     (qb_ref, kb_ref, vb_ref, gb_ref, yb_ref, sb_ref, cps - 1 - c, True)])


def _scan_specs(width, nb, rows, col=0):
    fwd = pl.BlockSpec((rows, width), lambda s, j: (s * nb + j, col))
    bwd = pl.BlockSpec((rows, width), lambda s, j: (s * nb + (nb - 1 - j), col))
    return fwd, bwd


def _gla_call(act, ga, n_seq, seq_len, cps):
    t = act.shape[0]
    rows = cps * CHUNK
    nb = seq_len // rows
    qf, qb = _scan_specs(GLA_KEY, nb, rows, COL_Q // GLA_KEY)
    kf, kb = _scan_specs(GLA_KEY, nb, rows, COL_K // GLA_KEY)
    vf, vb = _scan_specs(GLA_VAL, nb, rows, COL_V // GLA_VAL)
    yf, yb = _scan_specs(GLA_VAL, nb, rows)
    gf, _ = _scan_specs(GLA_KEY, nb, rows, COL_GF // GLA_KEY)
    _, gb = _scan_specs(GLA_KEY, nb, rows, COL_GB // GLA_KEY)
    state = pltpu.VMEM((GLA_HEADS, GLA_DK, GLA_DV), F32)
    return pl.pallas_call(
        functools.partial(_gla_kernel, cps=cps),
        grid=(n_seq, nb),
        in_specs=[qf, kf, vf, gf, qb, kb, vb, gb],
        out_specs=(yf, yb),
        out_shape=(jax.ShapeDtypeStruct((t, GLA_VAL), BF16),) * 2,
        scratch_shapes=[state, state],
        name="gla",
        compiler_params=pltpu.CompilerParams(dimension_semantics=("arbitrary", "arbitrary"),
                                             vmem_limit_bytes=VMEM_LIMIT),
    )(act, act, act, ga, act, act, act, ga)


def _ssd_chunks(jobs, hmask_ref):
    gw = SSD_HPG * SSD_HEADDIM
    work = []
    for xbc_ref, aux_ref, sel_ref, y_ref, h_ref, c, reverse in jobs:
        rows = slice(c * CHUNK, (c + 1) * CHUNK)
        off = SSD_HEADS if reverse else 0
        hi, lo = _split2(aux_ref[rows, :])
        hl = jnp.concatenate([hi, lo], axis=1)
        aux_r = hi.astype(F32) + lo.astype(F32)
        aux_t = aux_r.T
        ldt_j = aux_t[AUX_LDT + off:AUX_LDT + off + SSD_HEADS]
        rel_j = ldt_j - aux_t[AUX_CUM + off:AUX_CUM + off + SSD_HEADS]
        for g in range(SSD_GROUPS):
            bg = xbc_ref[rows, SSD_WIDTH + g * SSD_STATE:SSD_WIDTH + (g + 1) * SSD_STATE]
            cg = xbc_ref[rows, SSD_WIDTH + SSD_GN + g * SSD_STATE:SSD_WIDTH + SSD_GN + (g + 1) * SSD_STATE]
            work.append(dict(cb=_dot_nt(cg, bg), bg=bg, cg=cg, g=g, rows=rows, reverse=reverse, off=off,
                             hl=hl, aux_r=aux_r, ldt_j=ldt_j, rel_j=rel_j, sel_ref=sel_ref,
                             xbc_ref=xbc_ref, y_ref=y_ref, h_ref=h_ref))
    def expand(w):
        lo_col = w["g"] * gw
        return (_dot(w["hl"], w["sel_ref"][:, SEL_ECUM + lo_col:SEL_ECUM + lo_col + gw]),
                _dot(w["hl"], w["sel_ref"][:, SEL_DTAIL + lo_col:SEL_DTAIL + lo_col + gw]))

    expanded = {0: expand(work[0])}
    for idx, w in enumerate(work):
        if idx + 1 < len(work):
            expanded[idx + 1] = expand(work[idx + 1])
        ecum, dtail = expanded.pop(idx)
        g, off = w["g"], w["off"]
        gs = slice(g * gw, (g + 1) * gw)
        end = 0 if w["reverse"] else CHUNK - 1
        cb = jnp.where(_pair_mask(w["reverse"]), w["cb"], 0.0)
        ws = []
        for r in range(SSD_HPG):
            hd = g * SSD_HPG + r
            cum_i = jnp.broadcast_to(w["aux_r"][:, AUX_CUM + off + hd:AUX_CUM + off + hd + 1], (CHUNK, CHUNK))
            e = jnp.minimum(cum_i + w["rel_j"][hd:hd + 1], w["ldt_j"][hd:hd + 1])
            ws.append((cb * jnp.exp2(e)).astype(BF16))
        xg = w["xbc_ref"][w["rows"], gs]
        xbd = jnp.concatenate([xg] * SSD_HPG, axis=0) * hmask_ref[...]
        hg = w["h_ref"][g]
        y = _dot(jnp.concatenate(ws, axis=1), xbd) + _dot(w["cg"], hg.astype(BF16)) * ecum
        w["y_ref"][w["rows"], gs] = y.astype(BF16)
        xt = (xg.astype(F32) * dtail).astype(BF16)
        w["h_ref"][g] = hg * ecum[end:end + 1] + _dot_tn(w["bg"], xt)


def _ssd_kernel(xf_ref, af_ref, xb_ref, ab_ref, self_ref, selb_ref, hmask_ref,
                yf_ref, yb_ref, hf_ref, hb_ref, *, cps):
    @pl.when(pl.program_id(1) == 0)
    def _():
        hf_ref[...] = jnp.zeros_like(hf_ref)
        hb_ref[...] = jnp.zeros_like(hb_ref)

    for c in range(cps):
        _ssd_chunks([(xf_ref, af_ref, self_ref, yf_ref, hf_ref, c, False),
                     (xb_ref, ab_ref, selb_ref, yb_ref, hb_ref, cps - 1 - c, True)], hmask_ref)


def _ssd_call(xbc, ga, sel_f, sel_b, hmask, n_seq, seq_len, cps):
    t = xbc.shape[0]
    rows = cps * CHUNK
    nb = seq_len // rows
    xf, xb = _scan_specs(SSD_XBC, nb, rows)
    af, ab = _scan_specs(LANES, nb, rows, COL_AUX // LANES)
    yf, yb = _scan_specs(SSD_WIDTH, nb, rows)
    state = pltpu.VMEM((SSD_GROUPS, SSD_STATE, SSD_HPG * SSD_HEADDIM), F32)
    return pl.pallas_call(
        functools.partial(_ssd_kernel, cps=cps),
        grid=(n_seq, nb),
        in_specs=[xf, af, xb, ab, _const_spec(sel_f.shape), _const_spec(sel_b.shape),
                  _const_spec(hmask.shape)],
        out_specs=(yf, yb),
        out_shape=(jax.ShapeDtypeStruct((t, SSD_WIDTH), BF16),) * 2,
        scratch_shapes=[state, state],
        name="ssd",
        compiler_params=pltpu.CompilerParams(dimension_semantics=("arbitrary", "arbitrary"),
                                             vmem_limit_bytes=VMEM_LIMIT),
    )(xbc, ga, xbc, ga, sel_f, sel_b, hmask)


def _merge_kernel(h_ref, ygf_ref, ygb_ref, ysf_ref, ysb_ref, sg_ref, sz_ref, xs_ref, p_ref,
                  og_ref, dsk_ref, sng_ref, wo_ref, wpg_ref, wpp_ref, png_ref, fng_ref,
                  o_ref, *, final):
    o = ygf_ref[...].astype(F32) + ygb_ref[...].astype(F32)
    og = og_ref[...]
    heads = [_rms(o[:, h * GLA_DV:(h + 1) * GLA_DV], og) for h in range(GLA_HEADS)]
    o = jnp.concatenate(heads, axis=1) * sg_ref[...].astype(F32)
    y = ysf_ref[...].astype(F32) + ysb_ref[...].astype(F32) + dsk_ref[...] * xs_ref[...].astype(F32)
    y = _rms(y * sz_ref[...].astype(F32), sng_ref[...])
    h = h_ref[...] + _dot(o.astype(BF16), wo_ref[:GLA_VAL, :]) + _dot(y.astype(BF16), wo_ref[GLA_VAL:, :])
    e = _rms(_dot(p_ref[...].astype(BF16), wpp_ref[...]), png_ref[...])
    h = h + _sigmoid(_dot(h.astype(BF16), wpg_ref[...])) * e
    if final:
        h = _rms(h, fng_ref[...])
    o_ref[...] = h


def _merge_call(h, ygf, ygb, ysf, ysb, act, xbc, p, layer, og, dsk, sng, wo, wpg, wpp, png, fng, tm, final):
    t = h.shape[0]
    row = lambda width, col=0: pl.BlockSpec((tm, width), lambda i: (i, col))
    p_spec = pl.BlockSpec((None, tm, D_PLE), lambda i: (layer, i, 0))
    consts = (og, dsk, sng, wo, wpg, wpp, png, fng)
    return pl.pallas_call(
        functools.partial(_merge_kernel, final=final),
        grid=(t // tm,),
        in_specs=[row(D_MODEL)] + [row(GLA_VAL)] * 2 + [row(SSD_WIDTH)] * 2
                 + [row(GLA_VAL, COL_ZG // GLA_VAL), row(SSD_WIDTH, COL_ZS // SSD_WIDTH), row(SSD_WIDTH), p_spec]
                 + [_const_spec(c.shape) for c in consts],
        out_specs=row(D_MODEL),
        out_shape=jax.ShapeDtypeStruct((t, D_MODEL), F32),
        name="merge",
        compiler_params=pltpu.CompilerParams(dimension_semantics=("parallel",),
                                             vmem_limit_bytes=VMEM_LIMIT),
    )(h, ygf, ygb, ysf, ysb, act, act, xbc, p, *consts)


def _scan_constants():
    r = np.arange(CHUNK)
    tril = (r[None, :] <= r[:, None]).astype(np.float32)
    triu = (r[None, :] >= r[:, None]).astype(np.float32)
    tril2 = jnp.asarray(np.concatenate([tril, tril], axis=1), BF16)
    triu2 = jnp.asarray(np.concatenate([triu, triu], axis=1), BF16)
    sels = []
    for off in (0, SSD_HEADS):
        sel = np.zeros((LANES, SEL_WIDTH), np.float32)
        for hd in range(SSD_HEADS):
            sel[AUX_ECUM + off + hd, SEL_ECUM + hd * SSD_HEADDIM:SEL_ECUM + (hd + 1) * SSD_HEADDIM] = 1.0
            sel[AUX_DTAIL + off + hd, SEL_DTAIL + hd * SSD_HEADDIM:SEL_DTAIL + (hd + 1) * SSD_HEADDIM] = 1.0
        sels.append(jnp.asarray(np.concatenate([sel, sel], axis=0), BF16))
    gw = SSD_HPG * SSD_HEADDIM
    hmask = np.zeros((SSD_HPG * CHUNK, gw), np.float32)
    for rr in range(SSD_HPG):
        hmask[rr * CHUNK:(rr + 1) * CHUNK, rr * SSD_HEADDIM:(rr + 1) * SSD_HEADDIM] = 1.0
    return tril2, triu2, sels[0], sels[1], jnp.asarray(hmask, BF16)


def _layer_params(i, norm_g, w_in, w_gla_gate, b_gla_gate, gla_onorm_g, conv_w, conv_b, dt_bias,
                  a_log, d_skip, ssd_norm_g, w_out, w_ple_gate, w_ple_proj, ple_norm_g):
    reps = LANES // (N_DIR * SSD_HEADS)
    lr_slab = jnp.pad(w_in[i, :, SRC_LR:SRC_ZS], ((0, 0), (0, LANES - N_DIR * GLA_RANK)))
    dt_slab = jnp.tile(w_in[i, :, SRC_DT:], (1, reps))
    wg = jnp.zeros((LANES, N_DIR * GLA_KEY), F32)
    wg = wg.at[:GLA_RANK, :GLA_KEY].set(w_gla_gate[i, 0])
    wg = wg.at[GLA_RANK:2 * GLA_RANK, GLA_KEY:].set(w_gla_gate[i, 1])
    return dict(
        ng=norm_g[i][None, :], wsl=jnp.concatenate([lr_slab, dt_slab], axis=1),
        wg=wg.astype(BF16), bg=b_gla_gate[i].reshape(1, -1),
        dtb=jnp.tile(dt_bias[i].reshape(1, -1), (1, reps)), alog=jnp.tile(a_log[i].reshape(1, -1), (1, reps)),
        cw=conv_w[i], cb=conv_b[i][None, :], og=gla_onorm_g[i][None, :],
        dsk=jnp.repeat(d_skip[i], SSD_HEADDIM)[None, :], sng=ssd_norm_g[i][None, :],
        wo=w_out[i].astype(BF16), wpg=w_ple_gate[i].astype(BF16), wpp=w_ple_proj[i].astype(BF16),
        png=ple_norm_g[i][None, :])


def _tile(seq_len, want):
    tm = min(want, seq_len)
    assert seq_len % tm == 0 and tm % CHUNK == 0
    return tm


def _trunk(x, p, w_packed, layers, fng, consts):
    n_seq, seq_len, _ = x.shape
    p = p.reshape(p.shape[0], n_seq * seq_len, D_PLE)
    tril2, triu2, sel_f, sel_b, hmask = consts
    h = x.reshape(n_seq * seq_len, D_MODEL)
    tm = _tile(seq_len, 512)
    cps = min(SCAN_CHUNKS_PER_STEP, seq_len // CHUNK)
    depth = len(layers)
    for i, lp in enumerate(layers):
        act, xbc, ga = _proj_call(
            h, i, lp["ng"], w_packed, lp["wsl"], lp["wg"], lp["bg"], lp["dtb"], lp["alog"], tril2, triu2,
            lp["cw"], lp["cb"], tm, seq_len)
        ygf, ygb = _gla_call(act, ga, n_seq, seq_len, cps)
        ysf, ysb = _ssd_call(xbc, ga, sel_f, sel_b, hmask, n_seq, seq_len, cps)
        h = _merge_call(h, ygf, ygb, ysf, ysb, act, xbc, p, i,
                        lp["og"], lp["dsk"], lp["sng"], lp["wo"], lp["wpg"], lp["wpp"], lp["png"],
                        fng, tm, final=(i == depth - 1))
    return h.reshape(n_seq, seq_len, D_MODEL)


def kernel(x_prompt, x_sample, p_prompt, p_sample, norm_g, w_in, w_gla_gate, b_gla_gate, gla_onorm_g,
           conv_w, conv_b, dt_bias, a_log, d_skip, ssd_norm_g, w_out, w_ple_gate, w_ple_proj,
           ple_norm_g, final_norm_g):
    depth = w_in.shape[0]
    w_packed = _pack_call(w_in)
    layers = [_layer_params(i, norm_g, w_in, w_gla_gate, b_gla_gate, gla_onorm_g, conv_w, conv_b,
                            dt_bias, a_log, d_skip, ssd_norm_g, w_out, w_ple_gate, w_ple_proj,
                            ple_norm_g) for i in range(depth)]
    consts = _scan_constants()
    fng = final_norm_g[None, :]
    return (_trunk(x_prompt, p_prompt, w_packed, layers, fng, consts),
            _trunk(x_sample, p_sample, w_packed, layers, fng, consts))
```

```python
import functools

import numpy as np
import jax
import jax.numpy as jnp
from jax import lax
from jax.experimental import pallas as pl
from jax.experimental.pallas import tpu as pltpu

F32 = jnp.float32
BF16 = jnp.bfloat16

D_MODEL = 1024
D_PLE = 256
EPS = 1e-6
N_DIR = 2
GLA_HEADS = 4
GLA_KEY = 512
GLA_VAL = 1024
GLA_DK = 128
GLA_DV = 256
GLA_RANK = 16
GLA_TAU = 16.0
SSD_WIDTH = 1024
SSD_HEADDIM = 64
SSD_HEADS = 16
SSD_GROUPS = 4
SSD_HPG = 4
SSD_STATE = 128
SSD_CONV = 5
SSD_GN = SSD_GROUPS * SSD_STATE
SSD_XBC = SSD_WIDTH + 2 * SSD_GN
MIX_WIDTH = GLA_VAL + SSD_WIDTH

LANES = 128
CHUNK = 128
SCAN_CHUNKS_PER_STEP = 4
CONV_SLAB = 256
CONV_ROWS = 128
VMEM_LIMIT = 58 * 1024 * 1024

COL_Q = 0
COL_K = COL_Q + GLA_KEY
COL_V = COL_K + GLA_KEY
COL_ZG = COL_V + GLA_VAL
COL_ZS = COL_ZG + GLA_VAL
COL_XBC = COL_ZS + SSD_WIDTH
PROJ_WIDTH = COL_XBC + SSD_XBC
SRC_LR = 2 * GLA_KEY + 2 * GLA_VAL
SRC_ZS = SRC_LR + N_DIR * GLA_RANK
SRC_DT = SRC_ZS + SSD_WIDTH + SSD_XBC

COL_GF = 0
COL_GB = COL_GF + GLA_KEY
COL_AUX = COL_GB + GLA_KEY
GA_WIDTH = COL_AUX + LANES

AUX_CUM, AUX_LDT, AUX_ECUM, AUX_DTAIL = 0, 32, 64, 96
LOG2E = 1.4426950408889634
MIN_LOG2 = -126.0
SEL_ECUM = 0
SEL_DTAIL = SEL_ECUM + SSD_WIDTH
SEL_WIDTH = SEL_DTAIL + SSD_WIDTH


def _sigmoid(x):
    return 1.0 / (1.0 + jnp.exp2(x * (-LOG2E)))


def _silu(x):
    return x * _sigmoid(x)


def _log1pexp_neg_abs(x):
    return jnp.log(1.0 + jnp.exp(-jnp.abs(x)))


def _log_sigmoid(x):
    return jnp.minimum(x, 0.0) - _log1pexp_neg_abs(x)


def _softplus(x):
    return jnp.maximum(x, 0.0) + _log1pexp_neg_abs(x)


def _rms(x, g):
    ms = jnp.mean(x * x, axis=-1, keepdims=True)
    return x * lax.rsqrt(ms + EPS) * g


def _split2(x):
    hi = x.astype(BF16)
    lo = (x - hi.astype(F32)).astype(BF16)
    return hi, lo


def _dot(a, b):
    return jnp.dot(a, b, preferred_element_type=F32)


def _dot_nt(a, b):
    return lax.dot_general(a, b, (((1,), (1,)), ((), ())), preferred_element_type=F32)


def _dot_tn(a, b):
    return lax.dot_general(a, b, (((0,), (0,)), ((), ())), preferred_element_type=F32)


def _const_spec(shape):
    nd = len(shape)
    return pl.BlockSpec(shape, lambda *_: (0,) * nd, pipeline_mode=pl.Buffered(1))


def _pack_kernel(w_ref, o_ref):
    w = w_ref[0]
    o_ref[0, :, :SRC_LR] = w[:, :SRC_LR].astype(BF16)
    o_ref[0, :, SRC_LR:] = w[:, SRC_ZS:SRC_DT].astype(BF16)


def _pack_call(w_in):
    depth, rows, width = w_in.shape
    tr = 256
    return pl.pallas_call(
        _pack_kernel,
        grid=(depth, rows // tr),
        in_specs=[pl.BlockSpec((1, tr, width), lambda l, r: (l, r, 0))],
        out_specs=pl.BlockSpec((1, tr, PROJ_WIDTH), lambda l, r: (l, r, 0)),
        out_shape=jax.ShapeDtypeStruct((depth, rows, PROJ_WIDTH), BF16),
        name="pack",
        compiler_params=pltpu.CompilerParams(dimension_semantics=("parallel", "parallel"),
                                             vmem_limit_bytes=VMEM_LIMIT),
    )(w_in)


def _normed_inputs(h_ref, ng_ref, wsl_ref, u_ref, up_ref, slab_ref):
    tm = h_ref.shape[0]
    hn = _rms(h_ref[...], ng_ref[...])
    u = hn.astype(BF16)
    u_ref[...] = u
    up_ref[...] = jnp.swapaxes(hn.reshape(8, tm // 8, D_MODEL), 0, 1).reshape(tm, D_MODEL).astype(BF16)
    slab_ref[...] = _dot(u, wsl_ref[...].astype(BF16))


def _proj_kernel(h0_ref, hnext_ref, ng_ref, w_ref, wsl_ref, wg_ref, bg_ref, dtb_ref, alog_ref, tril_ref,
                 triu_ref, cw_ref, cb_ref, act_ref, xbc_ref, ga_ref,
                 bufa_ref, bufb_ref, ua_ref, upa_ref, sla_ref, ub_ref, upb_ref, slb_ref, *, tiles_per_seq):
    i = pl.program_id(0)
    refs = (hnext_ref, ng_ref, w_ref, wsl_ref, wg_ref, bg_ref, dtb_ref, alog_ref, tril_ref, triu_ref,
            cw_ref, cb_ref, act_ref, xbc_ref, ga_ref)
    seq_start = (i % tiles_per_seq) == 0

    @pl.when(i == 0)
    def _():
        bufb_ref[...] = jnp.zeros(bufb_ref.shape, F32)
        _normed_inputs(h0_ref, ng_ref, wsl_ref, ua_ref, upa_ref, sla_ref)

    @pl.when(i % 2 == 0)
    def _():
        _proj_tile(*refs, bufa_ref, bufb_ref, ua_ref, upa_ref, sla_ref, ub_ref, upb_ref, slb_ref, seq_start)

    @pl.when(i % 2 == 1)
    def _():
        _proj_tile(*refs, bufb_ref, bufa_ref, ub_ref, upb_ref, slb_ref, ua_ref, upa_ref, sla_ref, seq_start)


def _proj_tile(hnext_ref, ng_ref, w_ref, wsl_ref, wg_ref, bg_ref, dtb_ref, alog_ref, tril_ref, triu_ref,
               cw_ref, cb_ref, act_ref, xbc_ref, ga_ref, cur_ref, prv_ref,
               u_ref, up_ref, slab_ref, unext_ref, upnext_ref, slabnext_ref, seq_start):
    tm = hnext_ref.shape[0]
    nv = tm // 8
    pad = (SSD_CONV - 1) // 2
    u = u_ref[...]
    u_perm = up_ref[...]

    def mm(a, b):
        return _dot(u, w_ref[:, a:b])

    slab = slab_ref[...]
    sub = lax.broadcasted_iota(jnp.int32, (8, CONV_SLAB), 0)
    for j in range(SSD_XBC // CONV_SLAB):
        cs = slice(j * CONV_SLAB, (j + 1) * CONV_SLAB)
        xp = _dot(u_perm, w_ref[:, COL_XBC + j * CONV_SLAB:COL_XBC + (j + 1) * CONV_SLAB])
        cur_ref[8 * pad:8 * pad + tm, cs] = xp
        for e in range(pad):
            src = nv - pad + e
            below = prv_ref[8 * pad + 8 * src + 7:8 * pad + 8 * src + 8, cs]
            moved = pltpu.roll(xp[8 * src:8 * src + 8], 1, axis=0)
            cur_ref[8 * e:8 * e + 8, cs] = jnp.where(sub == 0, jnp.where(seq_start, 0.0, below), moved)
            first = prv_ref[8 * pad + 8 * e:8 * pad + 8 * e + 8, cs]
            moved = pltpu.roll(first, 7, axis=0)
            above = xp[8 * e:8 * e + 1]
            prv_ref[8 * pad + tm + 8 * e:8 * pad + tm + 8 * e + 8, cs] = jnp.where(
                sub == 7, jnp.where(seq_start, 0.0, above), moved)

    lr = slab[:, :LANES].astype(BF16)
    la = _log_sigmoid(_dot(lr, wg_ref[...]) + bg_ref[...]) * (LOG2E / GLA_TAU)

    dt = _softplus(slab[:, LANES:] + dtb_ref[...])
    da = dt * (-LOG2E * jnp.exp(alog_ref[...]))
    lane = lax.broadcasted_iota(jnp.int32, (1, LANES), 1)
    is_fwd = (lane % 32) < SSD_HEADS
    grp = lane // 32

    tril = tril_ref[...]
    triu = triu_ref[...]
    for c in range(tm // CHUNK):
        sl = slice(c * CHUNK, (c + 1) * CHUNK)
        hi, lo = _split2(la[sl])
        st = jnp.concatenate([hi, lo], axis=0)
        ga_ref[sl, COL_GF:COL_GB] = _dot(tril, st[:, :GLA_KEY])
        ga_ref[sl, COL_GB:COL_AUX] = _dot(triu, st[:, GLA_KEY:])

        dhi, dlo = _split2(da[sl])
        dst = jnp.concatenate([dhi, dlo], axis=0)
        cum = jnp.where(is_fwd, _dot(tril, dst), _dot(triu, dst))
        tot = jnp.where(is_fwd, cum[CHUNK - 1:CHUNK], cum[0:1])
        dtc = dt[sl]
        aux = jnp.where(grp == 0, cum,
                        jnp.where(grp == 1, jnp.maximum(jnp.log2(dtc), MIN_LOG2),
                                  jnp.where(grp == 2, jnp.exp2(cum), dtc * jnp.exp2(tot - cum))))
        ga_ref[sl, COL_AUX:] = aux

    def conv_slab(j):
        cs = slice(j * CONV_SLAB, (j + 1) * CONV_SLAB)
        for r0 in range(0, tm, CONV_ROWS):
            acc = cb_ref[:, cs] + cw_ref[0:1, cs] * prv_ref[r0:r0 + CONV_ROWS, cs]
            for tap in range(1, SSD_CONV):
                acc = acc + cw_ref[tap:tap + 1, cs] * prv_ref[r0 + 8 * tap:r0 + 8 * tap + CONV_ROWS, cs]
            act = jnp.swapaxes(_silu(acc).reshape(CONV_ROWS // 8, 8, CONV_SLAB), 0, 1).astype(BF16)
            for s8 in range(8):
                xbc_ref[nv * s8 + r0 // 8:nv * s8 + (r0 + CONV_ROWS) // 8, cs] = act[s8]

    scale_q = lambda r: r * (GLA_DK ** -0.5)
    posts = [scale_q, None, None, None, _silu, _silu, _silu, _silu]
    width = COL_XBC // len(posts)
    assert len(posts) == SSD_XBC // CONV_SLAB and width == GLA_KEY
    for j, post in enumerate(posts):
        r = mm(j * width, (j + 1) * width)
        act_ref[:, j * width:(j + 1) * width] = (post(r) if post else r).astype(BF16)
        conv_slab(j)
        if j == len(posts) // 2 - 1:
            _normed_inputs(hnext_ref, ng_ref, wsl_ref, unext_ref, upnext_ref, slabnext_ref)


def _proj_call(h, layer, ng, w, wsl, wg, bg, dtb, alog, tril, triu, cw, cb, tm, seq_len):
    t = h.shape[0]
    nb = t // tm
    row = lambda width: pl.BlockSpec((tm, width), lambda i: (jnp.minimum(i, nb - 1), 0))
    h_first = pl.BlockSpec((tm, D_MODEL), lambda i: (0, 0), pipeline_mode=pl.Buffered(1))
    h_next = pl.BlockSpec((tm, D_MODEL), lambda i: (jnp.minimum(i + 1, nb - 1), 0))
    conv_row = pl.BlockSpec((tm, SSD_XBC), lambda i: (jnp.maximum(i - 1, 0), 0))
    consts = (wsl, wg, bg, dtb, alog, tril, triu, cw, cb)
    w_spec = pl.BlockSpec((None,) + w.shape[1:], lambda i: (layer, 0, 0), pipeline_mode=pl.Buffered(1))
    out_shape = (
        jax.ShapeDtypeStruct((t, COL_XBC), BF16), jax.ShapeDtypeStruct((t, SSD_XBC), BF16),
        jax.ShapeDtypeStruct((t, GA_WIDTH), F32),
    )
    return pl.pallas_call(
        functools.partial(_proj_kernel, tiles_per_seq=seq_len // tm),
        grid=(nb + 1,),
        in_specs=[h_first, h_next, _const_spec(ng.shape), w_spec] + [_const_spec(c.shape) for c in consts],
        out_specs=(row(COL_XBC), conv_row, row(GA_WIDTH)),
        out_shape=out_shape,
        scratch_shapes=[pltpu.VMEM((tm + 8 * (SSD_CONV - 1), SSD_XBC), F32)] * 2
                       + [pltpu.VMEM((tm, D_MODEL), BF16), pltpu.VMEM((tm, D_MODEL), BF16),
                          pltpu.VMEM((tm, 2 * LANES), F32)] * 2,
        name="proj",
        compiler_params=pltpu.CompilerParams(dimension_semantics=("arbitrary",),
                                             vmem_limit_bytes=VMEM_LIMIT),
    )(h, h, ng, w, *consts)


def _pair_mask(reverse):
    row = lax.broadcasted_iota(jnp.int32, (CHUNK, CHUNK), 0)
    col = lax.broadcasted_iota(jnp.int32, (CHUNK, CHUNK), 1)
    return (col > row) if reverse else (col <= row)


def _gla_chunks(jobs):
    half = CHUNK // 2
    work = []
    for q_ref, k_ref, v_ref, g_ref, y_ref, s_ref, c, reverse in jobs:
        rows = slice(c * CHUNK, (c + 1) * CHUNK)
        mid = half if reverse else half - 1
        end = 0 if reverse else CHUNK - 1
        for h in range(GLA_HEADS):
            ks = slice(h * GLA_DK, (h + 1) * GLA_DK)
            g = g_ref[rows, ks]
            ref = g[mid:mid + 1]
            gtot = g[end:end + 1]
            d = g - ref
            qt = q_ref[rows, ks].astype(F32) * jnp.exp2(d)
            kt = k_ref[rows, ks].astype(F32) * jnp.exp2(-d)
            att = _dot_nt(qt.astype(BF16), kt.astype(BF16))
            work.append(dict(att=att, qs=(qt * jnp.exp2(ref)).astype(BF16),
                             kk=(kt * jnp.exp2(gtot - ref)).astype(BF16), etot=jnp.exp2(gtot),
                             v_ref=v_ref, y_ref=y_ref, s_ref=s_ref, rows=rows, h=h, reverse=reverse))
    for w in work:
        vs = slice(w["h"] * GLA_DV, (w["h"] + 1) * GLA_DV)
        att = jnp.where(_pair_mask(w["reverse"]), w["att"], 0.0).astype(BF16)
        y = _dot(jnp.concatenate([att, w["qs"]], axis=1),
                 jnp.concatenate([w["v_ref"][w["rows"], vs], w["s_ref"][w["h"]].astype(BF16)], axis=0))
        w["y_ref"][w["rows"], vs] = y.astype(BF16)
    for w in work:
        vs = slice(w["h"] * GLA_DV, (w["h"] + 1) * GLA_DV)
        ecol = jnp.broadcast_to(w["etot"], (GLA_DK, GLA_DK)).T
        upd = _dot_tn(w["kk"], w["v_ref"][w["rows"], vs])
        w["s_ref"][w["h"]] = w["s_ref"][w["h"]] * jnp.concatenate([ecol, ecol], axis=1) + upd


def _gla_kernel(qf_ref, kf_ref, vf_ref, gf_ref, qb_ref, kb_ref, vb_ref, gb_ref,
                yf_ref, yb_ref, sf_ref, sb_ref, *, cps):
    @pl.when(pl.program_id(1) == 0)
    def _():
        sf_ref[...] = jnp.zeros_like(sf_ref)
        sb_ref[...] = jnp.zeros_like(sb_ref)

    for c in range(cps):
        _gla_chunks([(qf_ref, kf_ref, vf_ref, gf_ref, yf_ref, sf_ref, c, False),
                     (qb_ref, kb_ref, vb_ref, gb_ref, yb_ref, sb_ref, cps - 1 - c, True)])


def _scan_specs(width, nb, rows, col=0):
    fwd = pl.BlockSpec((rows, width), lambda s, j: (s * nb + j, col))
    bwd = pl.BlockSpec((rows, width), lambda s, j: (s * nb + (nb - 1 - j), col))
    return fwd, bwd


def _gla_call(act, ga, n_seq, seq_len, cps):
    t = act.shape[0]
    rows = cps * CHUNK
    nb = seq_len // rows
    qf, qb = _scan_specs(GLA_KEY, nb, rows, COL_Q // GLA_KEY)
    kf, kb = _scan_specs(GLA_KEY, nb, rows, COL_K // GLA_KEY)
    vf, vb = _scan_specs(GLA_VAL, nb, rows, COL_V // GLA_VAL)
    yf, yb = _scan_specs(GLA_VAL, nb, rows)
    gf, _ = _scan_specs(GLA_KEY, nb, rows, COL_GF // GLA_KEY)
    _, gb = _scan_specs(GLA_KEY, nb, rows, COL_GB // GLA_KEY)
    state = pltpu.VMEM((GLA_HEADS, GLA_DK, GLA_DV), F32)
    return pl.pallas_call(
        functools.partial(_gla_kernel, cps=cps),
        grid=(n_seq, nb),
        in_specs=[qf, kf, vf, gf, qb, kb, vb, gb],
        out_specs=(yf, yb),
        out_shape=(jax.ShapeDtypeStruct((t, GLA_VAL), BF16),) * 2,
        scratch_shapes=[state, state],
        name="gla",
        compiler_params=pltpu.CompilerParams(dimension_semantics=("arbitrary", "arbitrary"),
                                             vmem_limit_bytes=VMEM_LIMIT),
    )(act, act, act, ga, act, act, act, ga)


def _ssd_chunks(jobs, hmask_ref, dsk_ref):
    gw = SSD_HPG * SSD_HEADDIM
    work = []
    for xbc_ref, aux_ref, sel_ref, y_ref, h_ref, c, reverse in jobs:
        rows = slice(c * CHUNK, (c + 1) * CHUNK)
        off = SSD_HEADS if reverse else 0
        hi, lo = _split2(aux_ref[rows, :])
        hl = jnp.concatenate([hi, lo], axis=1)
        aux_r = hi.astype(F32) + lo.astype(F32)
        aux_t = aux_r.T
        ldt_j = aux_t[AUX_LDT + off:AUX_LDT + off + SSD_HEADS]
        rel_j = ldt_j - aux_t[AUX_CUM + off:AUX_CUM + off + SSD_HEADS]
        for g in range(SSD_GROUPS):
            bg = xbc_ref[rows, SSD_WIDTH + g * SSD_STATE:SSD_WIDTH + (g + 1) * SSD_STATE]
            cg = xbc_ref[rows, SSD_WIDTH + SSD_GN + g * SSD_STATE:SSD_WIDTH + SSD_GN + (g + 1) * SSD_STATE]
            work.append(dict(cb=_dot_nt(cg, bg), bg=bg, cg=cg, g=g, rows=rows, reverse=reverse, off=off,
                             hl=hl, aux_r=aux_r, ldt_j=ldt_j, rel_j=rel_j, sel_ref=sel_ref,
                             xbc_ref=xbc_ref, y_ref=y_ref, h_ref=h_ref))
    def expand(w):
        lo_col = w["g"] * gw
        return (_dot(w["hl"], w["sel_ref"][:, SEL_ECUM + lo_col:SEL_ECUM + lo_col + gw]),
                _dot(w["hl"], w["sel_ref"][:, SEL_DTAIL + lo_col:SEL_DTAIL + lo_col + gw]))

    expanded = {0: expand(work[0])}
    for idx, w in enumerate(work):
        if idx + 1 < len(work):
            expanded[idx + 1] = expand(work[idx + 1])
        ecum, dtail = expanded.pop(idx)
        g, off = w["g"], w["off"]
        gs = slice(g * gw, (g + 1) * gw)
        end = 0 if w["reverse"] else CHUNK - 1
        cb = jnp.where(_pair_mask(w["reverse"]), w["cb"], 0.0)
        ws = []
        for r in range(SSD_HPG):
            hd = g * SSD_HPG + r
            cum_i = jnp.broadcast_to(w["aux_r"][:, AUX_CUM + off + hd:AUX_CUM + off + hd + 1], (CHUNK, CHUNK))
            e = jnp.minimum(cum_i + w["rel_j"][hd:hd + 1], w["ldt_j"][hd:hd + 1])
            ws.append((cb * jnp.exp2(e)).astype(BF16))
        xg = w["xbc_ref"][w["rows"], gs]
        xbd = jnp.concatenate([xg] * SSD_HPG, axis=0) * hmask_ref[...]
        hg = w["h_ref"][g]
        y = _dot(jnp.concatenate(ws, axis=1), xbd) + _dot(w["cg"], hg.astype(BF16)) * ecum
        xf = xg.astype(F32)
        if not w["reverse"]:
            y = y + dsk_ref[:, gs] * xf
        w["y_ref"][w["rows"], gs] = y.astype(BF16)
        xt = (xf * dtail).astype(BF16)
        w["h_ref"][g] = hg * ecum[end:end + 1] + _dot_tn(w["bg"], xt)


def _ssd_kernel(xf_ref, af_ref, xb_ref, ab_ref, self_ref, selb_ref, hmask_ref, dsk_ref,
                yf_ref, yb_ref, hf_ref, hb_ref, *, cps):
    @pl.when(pl.program_id(1) == 0)
    def _():
        hf_ref[...] = jnp.zeros_like(hf_ref)
        hb_ref[...] = jnp.zeros_like(hb_ref)

    for c in range(cps):
        _ssd_chunks([(xf_ref, af_ref, self_ref, yf_ref, hf_ref, c, False),
                     (xb_ref, ab_ref, selb_ref, yb_ref, hb_ref, cps - 1 - c, True)], hmask_ref, dsk_ref)


def _ssd_call(xbc, ga, sel_f, sel_b, hmask, dsk, n_seq, seq_len, cps):
    t = xbc.shape[0]
    rows = cps * CHUNK
    nb = seq_len // rows
    xf, xb = _scan_specs(SSD_XBC, nb, rows)
    af, ab = _scan_specs(LANES, nb, rows, COL_AUX // LANES)
    yf, yb = _scan_specs(SSD_WIDTH, nb, rows)
    state = pltpu.VMEM((SSD_GROUPS, SSD_STATE, SSD_HPG * SSD_HEADDIM), F32)
    return pl.pallas_call(
        functools.partial(_ssd_kernel, cps=cps),
        grid=(n_seq, nb),
        in_specs=[xf, af, xb, ab, _const_spec(sel_f.shape), _const_spec(sel_b.shape),
                  _const_spec(hmask.shape), _const_spec(dsk.shape)],
        out_specs=(yf, yb),
        out_shape=(jax.ShapeDtypeStruct((t, SSD_WIDTH), BF16),) * 2,
        scratch_shapes=[state, state],
        name="ssd",
        compiler_params=pltpu.CompilerParams(dimension_semantics=("arbitrary", "arbitrary"),
                                             vmem_limit_bytes=VMEM_LIMIT),
    )(xbc, ga, xbc, ga, sel_f, sel_b, hmask, dsk)


def _merge_kernel(h_ref, ygf_ref, ygb_ref, ysf_ref, ysb_ref, sg_ref, sz_ref, p_ref,
                  og_ref, sng_ref, wo_ref, wpg_ref, wpp_ref, png_ref, fng_ref,
                  o_ref, *, final):
    o = ygf_ref[...].astype(F32) + ygb_ref[...].astype(F32)
    og = og_ref[...]
    heads = [_rms(o[:, h * GLA_DV:(h + 1) * GLA_DV], og) for h in range(GLA_HEADS)]
    o = jnp.concatenate(heads, axis=1) * sg_ref[...].astype(F32)
    y = ysf_ref[...].astype(F32) + ysb_ref[...].astype(F32)
    y = _rms(y * sz_ref[...].astype(F32), sng_ref[...])
    h = h_ref[...] + _dot(o.astype(BF16), wo_ref[:GLA_VAL, :]) + _dot(y.astype(BF16), wo_ref[GLA_VAL:, :])
    e = _rms(_dot(p_ref[...].astype(BF16), wpp_ref[...]), png_ref[...])
    h = h + _sigmoid(_dot(h.astype(BF16), wpg_ref[...])) * e
    if final:
        h = _rms(h, fng_ref[...])
    o_ref[...] = h


def _merge_call(h, ygf, ygb, ysf, ysb, act, p, layer, og, sng, wo, wpg, wpp, png, fng, tm, final):
    t = h.shape[0]
    row = lambda width, col=0: pl.BlockSpec((tm, width), lambda i: (i, col))
    p_spec = pl.BlockSpec((None, tm, D_PLE), lambda i: (layer, i, 0))
    consts = (og, sng, wo, wpg, wpp, png, fng)
    return pl.pallas_call(
        functools.partial(_merge_kernel, final=final),
        grid=(t // tm,),
        in_specs=[row(D_MODEL)] + [row(GLA_VAL)] * 2 + [row(SSD_WIDTH)] * 2
                 + [row(GLA_VAL, COL_ZG // GLA_VAL), row(SSD_WIDTH, COL_ZS // SSD_WIDTH), p_spec]
                 + [_const_spec(c.shape) for c in consts],
        out_specs=row(D_MODEL),
        out_shape=jax.ShapeDtypeStruct((t, D_MODEL), F32),
        name="merge",
        compiler_params=pltpu.CompilerParams(dimension_semantics=("parallel",),
                                             vmem_limit_bytes=VMEM_LIMIT),
    )(h, ygf, ygb, ysf, ysb, act, act, p, *consts)


def _scan_constants():
    r = np.arange(CHUNK)
    tril = (r[None, :] <= r[:, None]).astype(np.float32)
    triu = (r[None, :] >= r[:, None]).astype(np.float32)
    tril2 = jnp.asarray(np.concatenate([tril, tril], axis=1), BF16)
    triu2 = jnp.asarray(np.concatenate([triu, triu], axis=1), BF16)
    sels = []
    for off in (0, SSD_HEADS):
        sel = np.zeros((LANES, SEL_WIDTH), np.float32)
        for hd in range(SSD_HEADS):
            sel[AUX_ECUM + off + hd, SEL_ECUM + hd * SSD_HEADDIM:SEL_ECUM + (hd + 1) * SSD_HEADDIM] = 1.0
            sel[AUX_DTAIL + off + hd, SEL_DTAIL + hd * SSD_HEADDIM:SEL_DTAIL + (hd + 1) * SSD_HEADDIM] = 1.0
        sels.append(jnp.asarray(np.concatenate([sel, sel], axis=0), BF16))
    gw = SSD_HPG * SSD_HEADDIM
    hmask = np.zeros((SSD_HPG * CHUNK, gw), np.float32)
    for rr in range(SSD_HPG):
        hmask[rr * CHUNK:(rr + 1) * CHUNK, rr * SSD_HEADDIM:(rr + 1) * SSD_HEADDIM] = 1.0
    return tril2, triu2, sels[0], sels[1], jnp.asarray(hmask, BF16)


def _layer_params(i, norm_g, w_in, w_gla_gate, b_gla_gate, gla_onorm_g, conv_w, conv_b, dt_bias,
                  a_log, d_skip, ssd_norm_g, w_out, w_ple_gate, w_ple_proj, ple_norm_g):
    reps = LANES // (N_DIR * SSD_HEADS)
    lr_slab = jnp.pad(w_in[i, :, SRC_LR:SRC_ZS], ((0, 0), (0, LANES - N_DIR * GLA_RANK)))
    dt_slab = jnp.tile(w_in[i, :, SRC_DT:], (1, reps))
    wg = jnp.zeros((LANES, N_DIR * GLA_KEY), F32)
    wg = wg.at[:GLA_RANK, :GLA_KEY].set(w_gla_gate[i, 0])
    wg = wg.at[GLA_RANK:2 * GLA_RANK, GLA_KEY:].set(w_gla_gate[i, 1])
    return dict(
        ng=norm_g[i][None, :], wsl=jnp.concatenate([lr_slab, dt_slab], axis=1),
        wg=wg.astype(BF16), bg=b_gla_gate[i].reshape(1, -1),
        dtb=jnp.tile(dt_bias[i].reshape(1, -1), (1, reps)), alog=jnp.tile(a_log[i].reshape(1, -1), (1, reps)),
        cw=conv_w[i], cb=conv_b[i][None, :], og=gla_onorm_g[i][None, :],
        dsk=jnp.repeat(d_skip[i], SSD_HEADDIM)[None, :], sng=ssd_norm_g[i][None, :],
        wo=w_out[i].astype(BF16), wpg=w_ple_gate[i].astype(BF16), wpp=w_ple_proj[i].astype(BF16),
        png=ple_norm_g[i][None, :])


def _tile(seq_len, want):
    tm = min(want, seq_len)
    assert seq_len % tm == 0 and tm % CHUNK == 0
    return tm


def _trunk(x, p, w_packed, layers, fng, consts):
    n_seq, seq_len, _ = x.shape
    p = p.reshape(p.shape[0], n_seq * seq_len, D_PLE)
    tril2, triu2, sel_f, sel_b, hmask = consts
    h = x.reshape(n_seq * seq_len, D_MODEL)
    tm = _tile(seq_len, 512)
    cps = min(SCAN_CHUNKS_PER_STEP, seq_len // CHUNK)
    depth = len(layers)
    for i, lp in enumerate(layers):
        act, xbc, ga = _proj_call(
            h, i, lp["ng"], w_packed, lp["wsl"], lp["wg"], lp["bg"], lp["dtb"], lp["alog"], tril2, triu2,
            lp["cw"], lp["cb"], tm, seq_len)
        ygf, ygb = _gla_call(act, ga, n_seq, seq_len, cps)
        ysf, ysb = _ssd_call(xbc, ga, sel_f, sel_b, hmask, lp["dsk"], n_seq, seq_len, cps)
        h = _merge_call(h, ygf, ygb, ysf, ysb, act, p, i,
                        lp["og"], lp["sng"], lp["wo"], lp["wpg"], lp["wpp"], lp["png"],
                        fng, tm, final=(i == depth - 1))
    return h.reshape(n_seq, seq_len, D_MODEL)


def kernel(x_prompt, x_sample, p_prompt, p_sample, norm_g, w_in, w_gla_gate, b_gla_gate, gla_onorm_g,
           conv_w, conv_b, dt_bias, a_log, d_skip, ssd_norm_g, w_out, w_ple_gate, w_ple_proj,
           ple_norm_g, final_norm_g):
    depth = w_in.shape[0]
    w_packed = _pack_call(w_in)
    layers = [_layer_params(i, norm_g, w_in, w_gla_gate, b_gla_gate, gla_onorm_g, conv_w, conv_b,
                            dt_bias, a_log, d_skip, ssd_norm_g, w_out, w_ple_gate, w_ple_proj,
                            ple_norm_g) for i in range(depth)]
    consts = _scan_constants()
    fng = final_norm_g[None, :]
    return (_trunk(x_prompt, p_prompt, w_packed, layers, fng, consts),
            _trunk(x_sample, p_sample, w_packed, layers, fng, consts))
```

```python
import functools

import numpy as np
import jax
import jax.numpy as jnp
from jax import lax
from jax.experimental import pallas as pl
from jax.experimental.pallas import tpu as pltpu

F32 = jnp.float32
BF16 = jnp.bfloat16

D_MODEL = 1024
D_PLE = 256
EPS = 1e-6
N_DIR = 2
GLA_HEADS = 4
GLA_KEY = 512
GLA_VAL = 1024
GLA_DK = 128
GLA_DV = 256
GLA_RANK = 16
GLA_TAU = 16.0
SSD_WIDTH = 1024
SSD_HEADDIM = 64
SSD_HEADS = 16
SSD_GROUPS = 4
SSD_HPG = 4
SSD_STATE = 128
SSD_CONV = 5
SSD_GN = SSD_GROUPS * SSD_STATE
SSD_XBC = SSD_WIDTH + 2 * SSD_GN
MIX_WIDTH = GLA_VAL + SSD_WIDTH

LANES = 128
CHUNK = 128
SCAN_CHUNKS_PER_STEP = 4
CONV_SLAB = 256
CONV_ROWS = 128
VMEM_LIMIT = 58 * 1024 * 1024

COL_Q = 0
COL_K = COL_Q + GLA_KEY
COL_V = COL_K + GLA_KEY
COL_ZG = COL_V + GLA_VAL
COL_ZS = COL_ZG + GLA_VAL
COL_XBC = COL_ZS + SSD_WIDTH
PROJ_WIDTH = COL_XBC + SSD_XBC
SRC_LR = 2 * GLA_KEY + 2 * GLA_VAL
SRC_ZS = SRC_LR + N_DIR * GLA_RANK
SRC_DT = SRC_ZS + SSD_WIDTH + SSD_XBC

COL_GF = 0
COL_GB = COL_GF + GLA_KEY
COL_AUX = COL_GB + GLA_KEY
GA_WIDTH = COL_AUX + LANES

AUX_CUM, AUX_LDT, AUX_ECUM, AUX_DTAIL = 0, 32, 64, 96
LOG2E = 1.4426950408889634
MIN_LOG2 = -126.0
SEL_ECUM = 0
SEL_DTAIL = SEL_ECUM + SSD_WIDTH
SEL_WIDTH = SEL_DTAIL + SSD_WIDTH


def _sigmoid(x):
    return 1.0 / (1.0 + jnp.exp2(x * (-LOG2E)))


def _silu(x):
    return x * _sigmoid(x)


def _log1pexp_neg_abs(x):
    return jnp.log(1.0 + jnp.exp(-jnp.abs(x)))


def _log_sigmoid(x):
    return jnp.minimum(x, 0.0) - _log1pexp_neg_abs(x)


def _softplus(x):
    return jnp.maximum(x, 0.0) + _log1pexp_neg_abs(x)


def _rms(x, g):
    ms = jnp.mean(x * x, axis=-1, keepdims=True)
    return x * lax.rsqrt(ms + EPS) * g


def _split2(x):
    hi = x.astype(BF16)
    lo = (x - hi.astype(F32)).astype(BF16)
    return hi, lo


def _dot(a, b):
    return jnp.dot(a, b, preferred_element_type=F32)


def _dot_nt(a, b):
    return lax.dot_general(a, b, (((1,), (1,)), ((), ())), preferred_element_type=F32)


def _dot_tn(a, b):
    return lax.dot_general(a, b, (((0,), (0,)), ((), ())), preferred_element_type=F32)


def _const_spec(shape):
    nd = len(shape)
    return pl.BlockSpec(shape, lambda *_: (0,) * nd, pipeline_mode=pl.Buffered(1))


def _pack_kernel(w_ref, o_ref):
    w = w_ref[0]
    o_ref[0, :, :SRC_LR] = w[:, :SRC_LR].astype(BF16)
    o_ref[0, :, SRC_LR:] = w[:, SRC_ZS:SRC_DT].astype(BF16)


def _pack_call(w_in):
    depth, rows, width = w_in.shape
    tr = 256
    return pl.pallas_call(
        _pack_kernel,
        grid=(depth, rows // tr),
        in_specs=[pl.BlockSpec((1, tr, width), lambda l, r: (l, r, 0))],
        out_specs=pl.BlockSpec((1, tr, PROJ_WIDTH), lambda l, r: (l, r, 0)),
        out_shape=jax.ShapeDtypeStruct((depth, rows, PROJ_WIDTH), BF16),
        name="pack",
        compiler_params=pltpu.CompilerParams(dimension_semantics=("parallel", "parallel"),
                                             vmem_limit_bytes=VMEM_LIMIT),
    )(w_in)


def _normed_inputs(h_ref, ng_ref, wsl_ref, u_ref, up_ref, slab_ref):
    tm = h_ref.shape[0]
    hn = _rms(h_ref[...], ng_ref[...])
    u = hn.astype(BF16)
    u_ref[...] = u
    up_ref[...] = jnp.swapaxes(hn.reshape(8, tm // 8, D_MODEL), 0, 1).reshape(tm, D_MODEL).astype(BF16)
    slab_ref[...] = _dot(u, wsl_ref[...].astype(BF16))


def _proj_kernel(h0_ref, hnext_ref, ng_ref, w_ref, wsl_ref, wg_ref, bg_ref, dtb_ref, alog_ref, tril_ref,
                 triu_ref, cw_ref, cb_ref, act_ref, xbc_ref, ga_ref,
                 bufa_ref, bufb_ref, ua_ref, upa_ref, sla_ref, ub_ref, upb_ref, slb_ref, *, tiles_per_seq):
    i = pl.program_id(0)
    refs = (hnext_ref, ng_ref, w_ref, wsl_ref, wg_ref, bg_ref, dtb_ref, alog_ref, tril_ref, triu_ref,
            cw_ref, cb_ref, act_ref, xbc_ref, ga_ref)
    seq_start = (i % tiles_per_seq) == 0

    @pl.when(i == 0)
    def _():
        bufb_ref[...] = jnp.zeros(bufb_ref.shape, F32)
        _normed_inputs(h0_ref, ng_ref, wsl_ref, ua_ref, upa_ref, sla_ref)

    @pl.when(i % 2 == 0)
    def _():
        _proj_tile(*refs, bufa_ref, bufb_ref, ua_ref, upa_ref, sla_ref, ub_ref, upb_ref, slb_ref, seq_start)

    @pl.when(i % 2 == 1)
    def _():
        _proj_tile(*refs, bufb_ref, bufa_ref, ub_ref, upb_ref, slb_ref, ua_ref, upa_ref, sla_ref, seq_start)


def _proj_tile(hnext_ref, ng_ref, w_ref, wsl_ref, wg_ref, bg_ref, dtb_ref, alog_ref, tril_ref, triu_ref,
               cw_ref, cb_ref, act_ref, xbc_ref, ga_ref, cur_ref, prv_ref,
               u_ref, up_ref, slab_ref, unext_ref, upnext_ref, slabnext_ref, seq_start):
    tm = hnext_ref.shape[0]
    nv = tm // 8
    pad = (SSD_CONV - 1) // 2
    u = u_ref[...]
    u_perm = up_ref[...]

    def mm(a, b):
        return _dot(u, w_ref[:, a:b])

    slab = slab_ref[...]
    sub = lax.broadcasted_iota(jnp.int32, (8, CONV_SLAB), 0)
    for j in range(SSD_XBC // CONV_SLAB):
        cs = slice(j * CONV_SLAB, (j + 1) * CONV_SLAB)
        xp = _dot(u_perm, w_ref[:, COL_XBC + j * CONV_SLAB:COL_XBC + (j + 1) * CONV_SLAB])
        cur_ref[8 * pad:8 * pad + tm, cs] = xp
        for e in range(pad):
            src = nv - pad + e
            below = prv_ref[8 * pad + 8 * src + 7:8 * pad + 8 * src + 8, cs]
            moved = pltpu.roll(xp[8 * src:8 * src + 8], 1, axis=0)
            cur_ref[8 * e:8 * e + 8, cs] = jnp.where(sub == 0, jnp.where(seq_start, 0.0, below), moved)
            first = prv_ref[8 * pad + 8 * e:8 * pad + 8 * e + 8, cs]
            moved = pltpu.roll(first, 7, axis=0)
            above = xp[8 * e:8 * e + 1]
            prv_ref[8 * pad + tm + 8 * e:8 * pad + tm + 8 * e + 8, cs] = jnp.where(
                sub == 7, jnp.where(seq_start, 0.0, above), moved)

    lr = slab[:, :LANES].astype(BF16)
    la = _log_sigmoid(_dot(lr, wg_ref[...]) + bg_ref[...]) * (LOG2E / GLA_TAU)

    dt = _softplus(slab[:, LANES:] + dtb_ref[...])
    da = dt * (-LOG2E * jnp.exp(alog_ref[...]))
    lane = lax.broadcasted_iota(jnp.int32, (1, LANES), 1)
    is_fwd = (lane % 32) < SSD_HEADS
    grp = lane // 32

    tril = tril_ref[...]
    triu = triu_ref[...]
    for c in range(tm // CHUNK):
        sl = slice(c * CHUNK, (c + 1) * CHUNK)
        hi, lo = _split2(la[sl])
        st = jnp.concatenate([hi, lo], axis=0)
        ga_ref[sl, COL_GF:COL_GB] = _dot(tril, st[:, :GLA_KEY])
        ga_ref[sl, COL_GB:COL_AUX] = _dot(triu, st[:, GLA_KEY:])

        dhi, dlo = _split2(da[sl])
        dst = jnp.concatenate([dhi, dlo], axis=0)
        cum = jnp.where(is_fwd, _dot(tril, dst), _dot(triu, dst))
        tot = jnp.where(is_fwd, cum[CHUNK - 1:CHUNK], cum[0:1])
        dtc = dt[sl]
        aux = jnp.where(grp == 0, cum,
                        jnp.where(grp == 1, jnp.maximum(jnp.log2(dtc), MIN_LOG2),
                                  jnp.where(grp == 2, jnp.exp2(cum), dtc * jnp.exp2(tot - cum))))
        ga_ref[sl, COL_AUX:] = aux

    def conv_slab(j):
        cs = slice(j * CONV_SLAB, (j + 1) * CONV_SLAB)
        for r0 in range(0, tm, CONV_ROWS):
            acc = cb_ref[:, cs] + cw_ref[0:1, cs] * prv_ref[r0:r0 + CONV_ROWS, cs]
            for tap in range(1, SSD_CONV):
                acc = acc + cw_ref[tap:tap + 1, cs] * prv_ref[r0 + 8 * tap:r0 + 8 * tap + CONV_ROWS, cs]
            act = jnp.swapaxes(_silu(acc).reshape(CONV_ROWS // 8, 8, CONV_SLAB), 0, 1).astype(BF16)
            for s8 in range(8):
                xbc_ref[nv * s8 + r0 // 8:nv * s8 + (r0 + CONV_ROWS) // 8, cs] = act[s8]

    scale_q = lambda r: r * (GLA_DK ** -0.5)
    posts = [scale_q, None, None, None, _silu, _silu, _silu, _silu]
    width = COL_XBC // len(posts)
    assert len(posts) == SSD_XBC // CONV_SLAB and width == GLA_KEY
    for j, post in enumerate(posts):
        r = mm(j * width, (j + 1) * width)
        act_ref[:, j * width:(j + 1) * width] = (post(r) if post else r).astype(BF16)
        conv_slab(j)
        if j == len(posts) // 2 - 1:
            _normed_inputs(hnext_ref, ng_ref, wsl_ref, unext_ref, upnext_ref, slabnext_ref)


def _proj_call(h, layer, ng, w, wsl, wg, bg, dtb, alog, tril, triu, cw, cb, tm, seq_len):
    t = h.shape[0]
    nb = t // tm
    row = lambda width: pl.BlockSpec((tm, width), lambda i: (jnp.minimum(i, nb - 1), 0))
    h_first = pl.BlockSpec((tm, D_MODEL), lambda i: (0, 0), pipeline_mode=pl.Buffered(1))
    h_next = pl.BlockSpec((tm, D_MODEL), lambda i: (jnp.minimum(i + 1, nb - 1), 0))
    conv_row = pl.BlockSpec((tm, SSD_XBC), lambda i: (jnp.maximum(i - 1, 0), 0))
    consts = (wsl, wg, bg, dtb, alog, tril, triu, cw, cb)
    w_spec = pl.BlockSpec((None,) + w.shape[1:], lambda i: (layer, 0, 0), pipeline_mode=pl.Buffered(1))
    out_shape = (
        jax.ShapeDtypeStruct((t, COL_XBC), BF16), jax.ShapeDtypeStruct((t, SSD_XBC), BF16),
        jax.ShapeDtypeStruct((t, GA_WIDTH), F32),
    )
    return pl.pallas_call(
        functools.partial(_proj_kernel, tiles_per_seq=seq_len // tm),
        grid=(nb + 1,),
        in_specs=[h_first, h_next, _const_spec(ng.shape), w_spec] + [_const_spec(c.shape) for c in consts],
        out_specs=(row(COL_XBC), conv_row, row(GA_WIDTH)),
        out_shape=out_shape,
        scratch_shapes=[pltpu.VMEM((tm + 8 * (SSD_CONV - 1), SSD_XBC), F32)] * 2
                       + [pltpu.VMEM((tm, D_MODEL), BF16), pltpu.VMEM((tm, D_MODEL), BF16),
                          pltpu.VMEM((tm, 2 * LANES), F32)] * 2,
        name="proj",
        compiler_params=pltpu.CompilerParams(dimension_semantics=("arbitrary",),
                                             vmem_limit_bytes=VMEM_LIMIT),
    )(h, h, ng, w, *consts)


def _pair_mask(reverse):
    row = lax.broadcasted_iota(jnp.int32, (CHUNK, CHUNK), 0)
    col = lax.broadcasted_iota(jnp.int32, (CHUNK, CHUNK), 1)
    return (col > row) if reverse else (col <= row)


def _gla_chunks(jobs):
    half = CHUNK // 2
    work = []
    for q_ref, k_ref, v_ref, g_ref, y_ref, s_ref, c, reverse in jobs:
        rows = slice(c * CHUNK, (c + 1) * CHUNK)
        mid = half if reverse else half - 1
        end = 0 if reverse else CHUNK - 1
        for h in range(GLA_HEADS):
            ks = slice(h * GLA_DK, (h + 1) * GLA_DK)
            g = g_ref[rows, ks]
            ref = g[mid:mid + 1]
            gtot = g[end:end + 1]
            d = g - ref
            qt = q_ref[rows, ks].astype(F32) * jnp.exp2(d)
            kt = k_ref[rows, ks].astype(F32) * jnp.exp2(-d)
            att = _dot_nt(qt.astype(BF16), kt.astype(BF16))
            work.append(dict(att=att, qs=(qt * jnp.exp2(ref)).astype(BF16),
                             kk=(kt * jnp.exp2(gtot - ref)).astype(BF16), etot=jnp.exp2(gtot),
                             v_ref=v_ref, y_ref=y_ref, s_ref=s_ref, rows=rows, h=h, reverse=reverse))
    for w in work:
        vs = slice(w["h"] * GLA_DV, (w["h"] + 1) * GLA_DV)
        att = jnp.where(_pair_mask(w["reverse"]), w["att"], 0.0).astype(BF16)
        y = _dot(jnp.concatenate([att, w["qs"]], axis=1),
                 jnp.concatenate([w["v_ref"][w["rows"], vs], w["s_ref"][w["h"]].astype(BF16)], axis=0))
        w["y_ref"][w["rows"], vs] = y.astype(BF16)
    for w in work:
        vs = slice(w["h"] * GLA_DV, (w["h"] + 1) * GLA_DV)
        ecol = jnp.broadcast_to(w["etot"], (GLA_DK, GLA_DK)).T
        upd = _dot_tn(w["kk"], w["v_ref"][w["rows"], vs])
        w["s_ref"][w["h"]] = w["s_ref"][w["h"]] * jnp.concatenate([ecol, ecol], axis=1) + upd


def _scan_specs(width, nb, rows, col=0):
    fwd = pl.BlockSpec((rows, width), lambda s, j: (s * nb + j, col))
    bwd = pl.BlockSpec((rows, width), lambda s, j: (s * nb + (nb - 1 - j), col))
    return fwd, bwd


def _ssd_chunks(jobs, hmask_ref, dsk_ref):
    gw = SSD_HPG * SSD_HEADDIM
    work = []
    for xbc_ref, aux_ref, sel_ref, y_ref, h_ref, c, reverse in jobs:
        rows = slice(c * CHUNK, (c + 1) * CHUNK)
        off = SSD_HEADS if reverse else 0
        hi, lo = _split2(aux_ref[rows, :])
        hl = jnp.concatenate([hi, lo], axis=1)
        aux_r = hi.astype(F32) + lo.astype(F32)
        aux_t = aux_r.T
        ldt_j = aux_t[AUX_LDT + off:AUX_LDT + off + SSD_HEADS]
        rel_j = ldt_j - aux_t[AUX_CUM + off:AUX_CUM + off + SSD_HEADS]
        for g in range(SSD_GROUPS):
            bg = xbc_ref[rows, SSD_WIDTH + g * SSD_STATE:SSD_WIDTH + (g + 1) * SSD_STATE]
            cg = xbc_ref[rows, SSD_WIDTH + SSD_GN + g * SSD_STATE:SSD_WIDTH + SSD_GN + (g + 1) * SSD_STATE]
            work.append(dict(cb=_dot_nt(cg, bg), bg=bg, cg=cg, g=g, rows=rows, reverse=reverse, off=off,
                             hl=hl, aux_r=aux_r, ldt_j=ldt_j, rel_j=rel_j, sel_ref=sel_ref,
                             xbc_ref=xbc_ref, y_ref=y_ref, h_ref=h_ref))
    def expand(w):
        lo_col = w["g"] * gw
        return (_dot(w["hl"], w["sel_ref"][:, SEL_ECUM + lo_col:SEL_ECUM + lo_col + gw]),
                _dot(w["hl"], w["sel_ref"][:, SEL_DTAIL + lo_col:SEL_DTAIL + lo_col + gw]))

    expanded = {0: expand(work[0])}
    for idx, w in enumerate(work):
        if idx + 1 < len(work):
            expanded[idx + 1] = expand(work[idx + 1])
        ecum, dtail = expanded.pop(idx)
        g, off = w["g"], w["off"]
        gs = slice(g * gw, (g + 1) * gw)
        end = 0 if w["reverse"] else CHUNK - 1
        cb = jnp.where(_pair_mask(w["reverse"]), w["cb"], 0.0)
        ws = []
        for r in range(SSD_HPG):
            hd = g * SSD_HPG + r
            cum_i = jnp.broadcast_to(w["aux_r"][:, AUX_CUM + off + hd:AUX_CUM + off + hd + 1], (CHUNK, CHUNK))
            e = jnp.minimum(cum_i + w["rel_j"][hd:hd + 1], w["ldt_j"][hd:hd + 1])
            ws.append((cb * jnp.exp2(e)).astype(BF16))
        xg = w["xbc_ref"][w["rows"], gs]
        xbd = jnp.concatenate([xg] * SSD_HPG, axis=0) * hmask_ref[...]
        hg = w["h_ref"][g]
        y = _dot(jnp.concatenate(ws, axis=1), xbd) + _dot(w["cg"], hg.astype(BF16)) * ecum
        xf = xg.astype(F32)
        if not w["reverse"]:
            y = y + dsk_ref[:, gs] * xf
        w["y_ref"][w["rows"], gs] = y.astype(BF16)
        xt = (xf * dtail).astype(BF16)
        w["h_ref"][g] = hg * ecum[end:end + 1] + _dot_tn(w["bg"], xt)


def _scan_kernel(qf_ref, kf_ref, vf_ref, gf_ref, qb_ref, kb_ref, vb_ref, gb_ref,
                 xf_ref, af_ref, xb_ref, ab_ref, self_ref, selb_ref, hmask_ref, dsk_ref,
                 ygf_ref, ygb_ref, ysf_ref, ysb_ref, sf_ref, sb_ref, hf_ref, hb_ref, *, cps):
    @pl.when(pl.program_id(1) == 0)
    def _():
        sf_ref[...] = jnp.zeros_like(sf_ref)
        sb_ref[...] = jnp.zeros_like(sb_ref)
        hf_ref[...] = jnp.zeros_like(hf_ref)
        hb_ref[...] = jnp.zeros_like(hb_ref)

    for c in range(cps):
        _gla_chunks([(qf_ref, kf_ref, vf_ref, gf_ref, ygf_ref, sf_ref, c, False),
                     (qb_ref, kb_ref, vb_ref, gb_ref, ygb_ref, sb_ref, cps - 1 - c, True)])
        _ssd_chunks([(xf_ref, af_ref, self_ref, ysf_ref, hf_ref, c, False),
                     (xb_ref, ab_ref, selb_ref, ysb_ref, hb_ref, cps - 1 - c, True)], hmask_ref, dsk_ref)


def _scan_call(act, xbc, ga, sel_f, sel_b, hmask, dsk, n_seq, seq_len, cps):
    t = act.shape[0]
    rows = cps * CHUNK
    nb = seq_len // rows
    qf, qb = _scan_specs(GLA_KEY, nb, rows, COL_Q // GLA_KEY)
    kf, kb = _scan_specs(GLA_KEY, nb, rows, COL_K // GLA_KEY)
    vf, vb = _scan_specs(GLA_VAL, nb, rows, COL_V // GLA_VAL)
    gf, _ = _scan_specs(GLA_KEY, nb, rows, COL_GF // GLA_KEY)
    _, gb = _scan_specs(GLA_KEY, nb, rows, COL_GB // GLA_KEY)
    xf, xb = _scan_specs(SSD_XBC, nb, rows)
    af, ab = _scan_specs(LANES, nb, rows, COL_AUX // LANES)
    yf, yb = _scan_specs(GLA_VAL, nb, rows)
    gla_state = pltpu.VMEM((GLA_HEADS, GLA_DK, GLA_DV), F32)
    ssd_state = pltpu.VMEM((SSD_GROUPS, SSD_STATE, SSD_HPG * SSD_HEADDIM), F32)
    return pl.pallas_call(
        functools.partial(_scan_kernel, cps=cps),
        grid=(n_seq, nb),
        in_specs=[qf, kf, vf, gf, qb, kb, vb, gb, xf, af, xb, ab, _const_spec(sel_f.shape),
                  _const_spec(sel_b.shape), _const_spec(hmask.shape), _const_spec(dsk.shape)],
        out_specs=(yf, yb, yf, yb),
        out_shape=(jax.ShapeDtypeStruct((t, GLA_VAL), BF16),) * 4,
        scratch_shapes=[gla_state, gla_state, ssd_state, ssd_state],
        name="scan",
        compiler_params=pltpu.CompilerParams(dimension_semantics=("arbitrary", "arbitrary"),
                                             vmem_limit_bytes=VMEM_LIMIT),
    )(act, act, act, ga, act, act, act, ga, xbc, ga, xbc, ga, sel_f, sel_b, hmask, dsk)


def _merge_kernel(h_ref, ygf_ref, ygb_ref, ysf_ref, ysb_ref, sg_ref, sz_ref, p_ref,
                  og_ref, sng_ref, wo_ref, wpg_ref, wpp_ref, png_ref, fng_ref,
                  o_ref, *, final):
    o = ygf_ref[...].astype(F32) + ygb_ref[...].astype(F32)
    og = og_ref[...]
    heads = [_rms(o[:, h * GLA_DV:(h + 1) * GLA_DV], og) for h in range(GLA_HEADS)]
    o = jnp.concatenate(heads, axis=1) * sg_ref[...].astype(F32)
    y = ysf_ref[...].astype(F32) + ysb_ref[...].astype(F32)
    y = _rms(y * sz_ref[...].astype(F32), sng_ref[...])
    h = h_ref[...] + _dot(o.astype(BF16), wo_ref[:GLA_VAL, :]) + _dot(y.astype(BF16), wo_ref[GLA_VAL:, :])
    e = _rms(_dot(p_ref[...].astype(BF16), wpp_ref[...]), png_ref[...])
    h = h + _sigmoid(_dot(h.astype(BF16), wpg_ref[...])) * e
    if final:
        h = _rms(h, fng_ref[...])
    o_ref[...] = h


def _merge_call(h, ygf, ygb, ysf, ysb, act, p, layer, og, sng, wo, wpg, wpp, png, fng, tm, final):
    t = h.shape[0]
    row = lambda width, col=0: pl.BlockSpec((tm, width), lambda i: (i, col))
    p_spec = pl.BlockSpec((None, tm, D_PLE), lambda i: (layer, i, 0))
    consts = (og, sng, wo, wpg, wpp, png, fng)
    return pl.pallas_call(
        functools.partial(_merge_kernel, final=final),
        grid=(t // tm,),
        in_specs=[row(D_MODEL)] + [row(GLA_VAL)] * 2 + [row(SSD_WIDTH)] * 2
                 + [row(GLA_VAL, COL_ZG // GLA_VAL), row(SSD_WIDTH, COL_ZS // SSD_WIDTH), p_spec]
                 + [_const_spec(c.shape) for c in consts],
        out_specs=row(D_MODEL),
        out_shape=jax.ShapeDtypeStruct((t, D_MODEL), F32),
        name="merge",
        compiler_params=pltpu.CompilerParams(dimension_semantics=("parallel",),
                                             vmem_limit_bytes=VMEM_LIMIT),
    )(h, ygf, ygb, ysf, ysb, act, act, p, *consts)


def _scan_constants():
    r = np.arange(CHUNK)
    tril = (r[None, :] <= r[:, None]).astype(np.float32)
    triu = (r[None, :] >= r[:, None]).astype(np.float32)
    tril2 = jnp.asarray(np.concatenate([tril, tril], axis=1), BF16)
    triu2 = jnp.asarray(np.concatenate([triu, triu], axis=1), BF16)
    sels = []
    for off in (0, SSD_HEADS):
        sel = np.zeros((LANES, SEL_WIDTH), np.float32)
        for hd in range(SSD_HEADS):
            sel[AUX_ECUM + off + hd, SEL_ECUM + hd * SSD_HEADDIM:SEL_ECUM + (hd + 1) * SSD_HEADDIM] = 1.0
            sel[AUX_DTAIL + off + hd, SEL_DTAIL + hd * SSD_HEADDIM:SEL_DTAIL + (hd + 1) * SSD_HEADDIM] = 1.0
        sels.append(jnp.asarray(np.concatenate([sel, sel], axis=0), BF16))
    gw = SSD_HPG * SSD_HEADDIM
    hmask = np.zeros((SSD_HPG * CHUNK, gw), np.float32)
    for rr in range(SSD_HPG):
        hmask[rr * CHUNK:(rr + 1) * CHUNK, rr * SSD_HEADDIM:(rr + 1) * SSD_HEADDIM] = 1.0
    return tril2, triu2, sels[0], sels[1], jnp.asarray(hmask, BF16)


def _layer_params(i, norm_g, w_in, w_gla_gate, b_gla_gate, gla_onorm_g, conv_w, conv_b, dt_bias,
                  a_log, d_skip, ssd_norm_g, w_out, w_ple_gate, w_ple_proj, ple_norm_g):
    reps = LANES // (N_DIR * SSD_HEADS)
    lr_slab = jnp.pad(w_in[i, :, SRC_LR:SRC_ZS], ((0, 0), (0, LANES - N_DIR * GLA_RANK)))
    dt_slab = jnp.tile(w_in[i, :, SRC_DT:], (1, reps))
    wg = jnp.zeros((LANES, N_DIR * GLA_KEY), F32)
    wg = wg.at[:GLA_RANK, :GLA_KEY].set(w_gla_gate[i, 0])
    wg = wg.at[GLA_RANK:2 * GLA_RANK, GLA_KEY:].set(w_gla_gate[i, 1])
    return dict(
        ng=norm_g[i][None, :], wsl=jnp.concatenate([lr_slab, dt_slab], axis=1),
        wg=wg.astype(BF16), bg=b_gla_gate[i].reshape(1, -1),
        dtb=jnp.tile(dt_bias[i].reshape(1, -1), (1, reps)), alog=jnp.tile(a_log[i].reshape(1, -1), (1, reps)),
        cw=conv_w[i], cb=conv_b[i][None, :], og=gla_onorm_g[i][None, :],
        dsk=jnp.repeat(d_skip[i], SSD_HEADDIM)[None, :], sng=ssd_norm_g[i][None, :],
        wo=w_out[i].astype(BF16), wpg=w_ple_gate[i].astype(BF16), wpp=w_ple_proj[i].astype(BF16),
        png=ple_norm_g[i][None, :])


def _tile(seq_len, want):
    tm = min(want, seq_len)
    assert seq_len % tm == 0 and tm % CHUNK == 0
    return tm


def _trunk(x, p, w_packed, layers, fng, consts):
    n_seq, seq_len, _ = x.shape
    p = p.reshape(p.shape[0], n_seq * seq_len, D_PLE)
    tril2, triu2, sel_f, sel_b, hmask = consts
    h = x.reshape(n_seq * seq_len, D_MODEL)
    tm = _tile(seq_len, 512)
    cps = min(SCAN_CHUNKS_PER_STEP, seq_len // CHUNK)
    depth = len(layers)
    for i, lp in enumerate(layers):
        act, xbc, ga = _proj_call(
            h, i, lp["ng"], w_packed, lp["wsl"], lp["wg"], lp["bg"], lp["dtb"], lp["alog"], tril2, triu2,
            lp["cw"], lp["cb"], tm, seq_len)
        ygf, ygb, ysf, ysb = _scan_call(act, xbc, ga, sel_f, sel_b, hmask, lp["dsk"], n_seq, seq_len, cps)
        h = _merge_call(h, ygf, ygb, ysf, ysb, act, p, i,
                        lp["og"], lp["sng"], lp["wo"], lp["wpg"], lp["wpp"], lp["png"],
                        fng, tm, final=(i == depth - 1))
    return h.reshape(n_seq, seq_len, D_MODEL)


def kernel(x_prompt, x_sample, p_prompt, p_sample, norm_g, w_in, w_gla_gate, b_gla_gate, gla_onorm_g,
           conv_w, conv_b, dt_bias, a_log, d_skip, ssd_norm_g, w_out, w_ple_gate, w_ple_proj,
           ple_norm_g, final_norm_g):
    depth = w_in.shape[0]
    w_packed = _pack_call(w_in)
    layers = [_layer_params(i, norm_g, w_in, w_gla_gate, b_gla_gate, gla_onorm_g, conv_w, conv_b,
                            dt_bias, a_log, d_skip, ssd_norm_g, w_out, w_ple_gate, w_ple_proj,
                            ple_norm_g) for i in range(depth)]
    consts = _scan_constants()
    fng = final_norm_g[None, :]
    return (_trunk(x_prompt, p_prompt, w_packed, layers, fng, consts),
            _trunk(x_sample, p_sample, w_packed, layers, fng, consts))
```
